```python
import jax
import jax.numpy as jnp
from jax import lax
import numpy as np

D_MODEL = 1024
BATCH = 32
SEQ = 256
DEPTH = 4
DEC_BATCH = 8
DEC_SEQ = 4096
PAST_LEN = 256

GRID_W = 64
CHUNK = 64
N_DIR = 2
N_EVEN = (DEPTH + 1) // 2
N_ODD = DEPTH // 2
ALPHA = (2.0 * DEPTH) ** 0.25
BETA = (8.0 * DEPTH) ** -0.25
H_A = 4
DK_A = 128
DV_A = 128
H_B = 4
DK_B = 128
DV_B = 128
W_A = H_A * DV_A
W_B = H_B * DV_B
EVEN_SIZES = (H_A * DK_A, H_A * DK_A, W_A, W_A, N_DIR * H_A, N_DIR * H_A, H_B * DK_B, H_B * DK_B, W_B, W_B)
EVEN_SPLITS = [int(s) for s in np.cumsum(EVEN_SIZES)[:-1]]
EVEN_IN = int(sum(EVEN_SIZES))
D_INNER = 2 * D_MODEL
P_C = 64
H_C = D_INNER // P_C
G_C = 4
J_C = H_C // G_C
N_C = 128
D_CONV = 5
CONV_DIM = D_INNER + 2 * G_C * N_C
ODD_IN = D_INNER + CONV_DIM + N_DIR * H_C
N_KEYS = 128
N_EXPERTS = N_KEYS * N_KEYS
H_P = 8
D_QK = 256
D_HALF = D_QK // 2
TOPK = 16
PEER_BLOCK = 128

kernel_name = 'bidir_mlstm_retnet_ssd_peer_prefix_trunk'


def _chunks(a):
    b, t = a.shape[0], a.shape[1]
    return jnp.moveaxis(a.reshape(b, t // CHUNK, CHUNK, *a.shape[2:]), 1, 0)


def _unchunks(a):
    a = jnp.moveaxis(a, 0, 1)
    return a.reshape(a.shape[0], a.shape[1] * a.shape[2], *a.shape[3:])


def _flip(a):
    return jnp.flip(a, axis=1)


def _layer_norm(x, g, b, eps=1e-5):
    xf = x.astype(jnp.float32)
    mu = xf.mean(-1, keepdims=True)
    var = jnp.square(xf - mu).mean(-1, keepdims=True)
    return ((xf - mu) * lax.rsqrt(var + eps) * g + b).astype(x.dtype)


def _head_norm(h, w, eps=1e-5):
    mu = h.mean(-1, keepdims=True)
    var = jnp.square(h - mu).mean(-1, keepdims=True)
    return (h - mu) * lax.rsqrt(var + eps) * w


def _modulation(cvec, w, b):
    m = jax.nn.silu(cvec) @ w + b
    return [t[:, None, :] for t in jnp.split(m, 6, axis=-1)]


def _grid_pos_embed(n_tok, dtype):
    rows = n_tok // GRID_W
    r = jnp.repeat(jnp.arange(rows, dtype=jnp.float32), GRID_W)
    col = jnp.tile(jnp.arange(GRID_W, dtype=jnp.float32), rows)
    quarter = D_MODEL // 4
    freqs = 1.0 / (10000.0 ** (jnp.arange(quarter, dtype=jnp.float32) / quarter))
    er = r[:, None] * freqs
    ec = col[:, None] * freqs
    return jnp.concatenate([jnp.sin(er), jnp.cos(er), jnp.sin(ec), jnp.cos(ec)], axis=-1).astype(dtype)


def _mlstm_scan(q, k, v, li, lf, C0, n0, m0):
    causal = jnp.tril(jnp.ones((CHUNK, CHUNK), dtype=bool))

    def step(carry, inp):
        C, n, m = carry
        qc, kc, vc, lic, lfc = inp
        b = jnp.cumsum(lfc, axis=1)
        dlog = jnp.where(causal[None, :, :, None], b[:, :, None] - b[:, None] + lic[:, None], -jnp.inf)
        inter = b + m[:, None]
        m_t = jnp.maximum(inter, dlog.max(axis=2))
        s = jnp.einsum('bthk,bshk->btsh', qc, kc) * jnp.exp(dlog - m_t[:, :, None])
        w_int = jnp.exp(inter - m_t)
        num = jnp.einsum('btsh,bshv->bthv', s, vc) + w_int[..., None] * jnp.einsum('bthk,bhkv->bthv', qc, C)
        den = s.sum(axis=2) + w_int * jnp.einsum('bthk,bhk->bth', qc, n)
        h = num / jnp.maximum(jnp.abs(den), jnp.exp(-m_t))[..., None]
        g = b[:, -1:] - b + lic
        m_new = jnp.maximum(b[:, -1] + m, g.max(axis=1))
        wg = jnp.exp(g - m_new[:, None])
        dec = jnp.exp(b[:, -1] + m - m_new)
        C_new = dec[..., None, None] * C + jnp.einsum('bsh,bshk,bshv->bhkv', wg, kc, vc)
        n_new = dec[..., None] * n + jnp.einsum('bsh,bshk->bhk', wg, kc)
        return (C_new, n_new, m_new), h

    fin, h = lax.scan(step, (C0, n0, m0), (_chunks(q), _chunks(k), _chunks(v), _chunks(li), _chunks(lf)))
    return _unchunks(h), fin


def _retention_scan(q, k, v, log_gamma, S0):
    idx = jnp.arange(CHUNK, dtype=jnp.float32)
    diff = idx[:, None] - idx[None, :]
    decay_mat = jnp.where((diff >= 0)[..., None], jnp.exp(jnp.maximum(diff, 0.0)[..., None] * log_gamma), 0.0)
    xi = jnp.exp((idx[:, None] + 1.0) * log_gamma)
    zeta = jnp.exp((CHUNK - 1.0 - idx)[:, None] * log_gamma)
    g_end = jnp.exp(CHUNK * log_gamma)

    def step(S, inp):
        qc, kc, vc = inp
        s = jnp.einsum('bthk,bshk->btsh', qc, kc) * decay_mat
        o = jnp.einsum('btsh,bshv->bthv', s, vc) + xi[None, :, :, None] * jnp.einsum('bthk,bhkv->bthv', qc, S)
        S_new = g_end[None, :, None, None] * S + jnp.einsum('bshk,bshv,sh->bhkv', kc, vc, zeta)
        return S_new, o

    fin, o = lax.scan(step, S0, (_chunks(q), _chunks(k), _chunks(v)))
    return _unchunks(o), fin


def _ssd_scan(x, bm, cm, dt, a, h0):
    causal = jnp.tril(jnp.ones((CHUNK, CHUNK), dtype=bool))

    def step(h, inp):
        xc, bc, cc, dtc, ac = inp
        acum = jnp.cumsum(ac, axis=1)
        seg = jnp.where(causal[None, :, :, None, None], acum[:, :, None] - acum[:, None], -jnp.inf)
        s = jnp.einsum('btgn,bsgn->btsg', cc, bc)[..., None] * jnp.exp(seg) * dtc[:, None]
        y = jnp.einsum('btsgj,bsgjp->btgjp', s, xc) + jnp.exp(acum)[..., None] * jnp.einsum('btgn,bgjpn->btgjp', cc, h)
        a_end = acum[:, -1]
        w = jnp.exp(a_end[:, None] - acum) * dtc
        h_new = jnp.exp(a_end)[..., None, None] * h + jnp.einsum('bsgj,bsgjp,bsgn->bgjpn', w, xc, bc)
        return h_new, y

    fin, y = lax.scan(step, h0, (_chunks(x), _chunks(bm), _chunks(cm), _chunks(dt), _chunks(a)))
    return _unchunks(y), fin


def _even_mixer(u, w_in, i_bias, f_bias, log_decay, gn_a, gn_b, w_out, C0, n0, m0, S0):
    f32 = jnp.float32
    bsz, t, _ = u.shape
    qa, ka, va, oa, ia, fa, qb, kb, vb, gb = jnp.split(u @ w_in, EVEN_SPLITS, axis=-1)
    qa = qa.reshape(bsz, t, H_A, DK_A).astype(f32)
    ka = ka.reshape(bsz, t, H_A, DK_A).astype(f32) * DK_A ** -0.5
    va = va.reshape(bsz, t, H_A, DV_A).astype(f32)
    li = ia.reshape(bsz, t, N_DIR, H_A).astype(f32) + i_bias.astype(f32)
    lf = jax.nn.log_sigmoid(fa.reshape(bsz, t, N_DIR, H_A).astype(f32) + f_bias.astype(f32))
    C0 = C0.astype(f32)
    n0 = n0.astype(f32)
    m0 = m0.astype(f32)
    S0 = S0.astype(f32)
    ha_f, (Cf, nf, mf) = _mlstm_scan(qa, ka, va, li[:, :, 0], lf[:, :, 0], C0[:, 0], n0[:, 0], m0[:, 0])
    ha_b, (Cb, nb, mb) = _mlstm_scan(_flip(qa), _flip(ka), _flip(va), _flip(li[:, :, 1]), _flip(lf[:, :, 1]), C0[:, 1], n0[:, 1], m0[:, 1])
    ha = _head_norm(ha_f + _flip(ha_b), gn_a) * jax.nn.sigmoid(oa.reshape(bsz, t, H_A, DV_A).astype(f32))
    qb = qb.reshape(bsz, t, H_B, DK_B).astype(f32) * DK_B ** -0.5
    kb = kb.reshape(bsz, t, H_B, DK_B).astype(f32)
    vb = vb.reshape(bsz, t, H_B, DV_B).astype(f32)
    lg = -jnp.exp(log_decay.astype(f32))
    hb_f, Sf = _retention_scan(qb, kb, vb, lg[0], S0[:, 0])
    hb_b, Sb = _retention_scan(_flip(qb), _flip(kb), _flip(vb), lg[1], S0[:, 1])
    hb = _head_norm(hb_f + _flip(hb_b), gn_b) * jax.nn.silu(gb.reshape(bsz, t, H_B, DV_B).astype(f32))
    mixed = jnp.concatenate([ha.reshape(bsz, t, W_A), hb.reshape(bsz, t, W_B)], axis=-1).astype(u.dtype)
    fin = (jnp.stack([Cf, Cb], axis=1), jnp.stack([nf, nb], axis=1), jnp.stack([mf, mb], axis=1), jnp.stack([Sf, Sb], axis=1))
    return mixed @ w_out, fin


def _odd_mixer(u, w_in, conv_w, conv_b, dt_bias, a_log, d_skip, norm_w, w_out, h_st):
    f32 = jnp.float32
    bsz, t, _ = u.shape
    z, xbc, dt = jnp.split(u @ w_in, [D_INNER, D_INNER + CONV_DIM], axis=-1)
    xbc = lax.conv_general_dilated(xbc, conv_w.astype(xbc.dtype)[:, None, :], window_strides=(1,), padding=[(D_CONV // 2, D_CONV // 2)], dimension_numbers=('NWC', 'WIO', 'NWC'), feature_group_count=CONV_DIM)
    xbc = jax.nn.silu(xbc + conv_b)
    xs, bm, cm = jnp.split(xbc, [D_INNER, D_INNER + G_C * N_C], axis=-1)
    xs = xs.reshape(bsz, t, G_C, J_C, P_C).astype(f32)
    bm = bm.reshape(bsz, t, G_C, N_C).astype(f32)
    cm = cm.reshape(bsz, t, G_C, N_C).astype(f32)
    dt = jax.nn.softplus(dt.reshape(bsz, t, N_DIR, G_C, J_C).astype(f32) + dt_bias.astype(f32).reshape(N_DIR, G_C, J_C))
    A = -jnp.exp(a_log.astype(f32)).reshape(N_DIR, G_C, J_C)
    h0 = h_st.astype(f32).reshape(bsz, N_DIR, G_C, J_C, P_C, N_C)
    y_f, hf = _ssd_scan(xs, bm, cm, dt[:, :, 0], dt[:, :, 0] * A[0], h0[:, 0])
    dt_b = _flip(dt[:, :, 1])
    y_b, hb = _ssd_scan(_flip(xs), _flip(bm), _flip(cm), dt_b, dt_b * A[1], h0[:, 1])
    y = y_f + _flip(y_b) + d_skip.astype(f32).reshape(G_C, J_C, 1) * xs
    y = y.reshape(bsz, t, D_INNER) * jax.nn.silu(z.astype(f32))
    y = y * lax.rsqrt(jnp.square(y).mean(-1, keepdims=True) + 1e-5) * norm_w
    fin = jnp.stack([hf.reshape(bsz, H_C, P_C, N_C), hb.reshape(bsz, H_C, P_C, N_C)], axis=1)
    return y.astype(u.dtype) @ w_out, fin


def _peer(x, w_q, sub_keys, u_tab, v_tab):
    bsz, t, d = x.shape
    xt = x.reshape(-1, PEER_BLOCK, d)

    def block(xb):
        q = (xb @ w_q).reshape(PEER_BLOCK, H_P, 2, D_HALF).astype(jnp.float32)
        s = jnp.einsum('thcd,hcnd->thcn', q, sub_keys.astype(jnp.float32))
        sv, si = lax.top_k(s, TOPK)
        cand = (sv[:, :, 0, :, None] + sv[:, :, 1, None, :]).reshape(PEER_BLOCK, H_P, TOPK * TOPK)
        cand_idx = (si[:, :, 0, :, None] * N_KEYS + si[:, :, 1, None, :]).reshape(PEER_BLOCK, H_P, TOPK * TOPK)
        best, pos = lax.top_k(cand, TOPK)
        idx = jnp.take_along_axis(cand_idx, pos, axis=-1)
        g = jax.nn.softmax(best, axis=-1)
        act = jax.nn.gelu(jnp.einsum('thkd,td->thk', u_tab[idx], xb).astype(jnp.float32), approximate=False)
        return jnp.einsum('thk,thkd->td', (g * act).astype(xb.dtype), v_tab[idx])

    return lax.map(block, xt).reshape(bsz, t, d)


def setup_inputs(seed: int = 0) -> dict:
    key = jax.random.key(seed)
    ks = iter(jax.random.split(key, 48))
    f32 = jnp.float32

    def nrm(shape, scale):
        return jax.random.normal(next(ks), shape, f32) * scale

    D = D_MODEL
    ret_base = jnp.log(-jnp.log(1.0 - 2.0 ** (-5.0 - jnp.arange(H_B, dtype=f32))))
    dt0 = jnp.exp(jax.random.uniform(next(ks), (N_ODD, N_DIR, H_C), f32, float(np.log(1e-3)), float(np.log(1e-1))))
    return {
        'x_prompt': nrm((BATCH, SEQ, D), 1.0),
        'x_sample': nrm((DEC_BATCH, DEC_SEQ, D), 1.0),
        'c': nrm((DEC_BATCH, D), 1.0),
        'c_ctx': nrm((D,), 1.0),
        'state_mlstm_C': nrm((DEC_BATCH, N_EVEN, N_DIR, H_A, DK_A, DV_A), 0.2),
        'state_mlstm_n': nrm((DEC_BATCH, N_EVEN, N_DIR, H_A, DK_A), 0.2),
        'state_mlstm_m': nrm((DEC_BATCH, N_EVEN, N_DIR, H_A), 1.0),
        'state_ret': nrm((DEC_BATCH, N_EVEN, N_DIR, H_B, DK_B, DV_B), 0.2),
        'state_ssd': nrm((DEC_BATCH, N_ODD, N_DIR, H_C, P_C, N_C), 0.2),
        'w_mod': nrm((DEPTH, D, 6 * D), D ** -0.5),
        'b_mod': nrm((DEPTH, 6 * D), 0.02),
        'ln1_g': 1.0 + nrm((DEPTH, D), 0.02),
        'ln1_b': nrm((DEPTH, D), 0.02),
        'ln2_g': 1.0 + nrm((DEPTH, D), 0.02),
        'ln2_b': nrm((DEPTH, D), 0.02),
        'even_w_in': nrm((N_EVEN, D, EVEN_IN), D ** -0.5),
        'mlstm_i_bias': nrm((N_EVEN, N_DIR, H_A), 0.1),
        'mlstm_f_bias': jnp.linspace(3.0, 6.0, H_A, dtype=f32) + nrm((N_EVEN, N_DIR, H_A), 0.1),
        'ret_log_decay': ret_base + nrm((N_EVEN, N_DIR, H_B), 0.05),
        'even_gn_a': 1.0 + nrm((N_EVEN, H_A, DV_A), 0.02),
        'even_gn_b': 1.0 + nrm((N_EVEN, H_B, DV_B), 0.02),
        'even_w_out': nrm((N_EVEN, W_A + W_B, D), BETA * (W_A + W_B) ** -0.5),
        'odd_w_in': nrm((N_ODD, D, ODD_IN), D ** -0.5),
        'odd_conv_w': nrm((N_ODD, D_CONV, CONV_DIM), D_CONV ** -0.5),
        'odd_conv_b': nrm((N_ODD, CONV_DIM), 0.02),
        'odd_dt_bias': dt0 + jnp.log(-jnp.expm1(-dt0)),
        'odd_a_log': jnp.log(jax.random.uniform(next(ks), (N_ODD, N_DIR, H_C), f32, 1.0, 16.0)),
        'odd_d': 1.0 + nrm((N_ODD, H_C), 0.1),
        'odd_norm_w': 1.0 + nrm((N_ODD, D_INNER), 0.02),
        'odd_w_out': nrm((N_ODD, D_INNER, D), BETA * D_INNER ** -0.5),
        'peer_w_q': nrm((DEPTH, D, H_P * D_QK), D ** -0.5),
        'peer_sub_keys': nrm((DEPTH, H_P, 2, N_KEYS, D_HALF), D_HALF ** -0.5),
        'peer_u': nrm((DEPTH, N_EXPERTS, D), D ** -0.5),
        'peer_v': nrm((DEPTH, N_EXPERTS, D), BETA),
    }


def reference(x_prompt, x_sample, c, c_ctx, state_mlstm_C, state_mlstm_n, state_mlstm_m, state_ret, state_ssd, w_mod, b_mod, ln1_g, ln1_b, ln2_g, ln2_b, even_w_in, mlstm_i_bias, mlstm_f_bias, ret_log_decay, even_gn_a, even_gn_b, even_w_out, odd_w_in, odd_conv_w, odd_conv_b, odd_dt_bias, odd_a_log, odd_d, odd_norm_w, odd_w_out, peer_w_q, peer_sub_keys, peer_u, peer_v):
    f32 = jnp.float32
    bp = x_prompt.shape[0]
    xp = x_prompt
    xs = x_sample + _grid_pos_embed(x_sample.shape[1], x_sample.dtype)[None]
    new_C, new_n, new_m, new_S, new_h = [], [], [], [], []
    for l in range(DEPTH):
        sh1p, sc1p, g1p, sh2p, sc2p, g2p = _modulation(c_ctx[None, :], w_mod[l], b_mod[l])
        sh1s, sc1s, g1s, sh2s, sc2s, g2s = _modulation(c, w_mod[l], b_mod[l])
        hp = xp * (1 + sc1p) + sh1p
        hs = xs * (1 + sc1s) + sh1s
        j = l // 2
        if l % 2 == 0:
            prm = (even_w_in[j], mlstm_i_bias[j], mlstm_f_bias[j], ret_log_decay[j], even_gn_a[j], even_gn_b[j], even_w_out[j])
            mp, fin = _even_mixer(hp, *prm, jnp.zeros((bp, N_DIR, H_A, DK_A, DV_A), f32), jnp.zeros((bp, N_DIR, H_A, DK_A), f32), jnp.zeros((bp, N_DIR, H_A), f32), jnp.zeros((bp, N_DIR, H_B, DK_B, DV_B), f32))
            ms, _ = _even_mixer(hs, *prm, state_mlstm_C[:, j], state_mlstm_n[:, j], state_mlstm_m[:, j], state_ret[:, j])
            new_C.append(fin[0])
            new_n.append(fin[1])
            new_m.append(fin[2])
            new_S.append(fin[3])
        else:
            prm = (odd_w_in[j], odd_conv_w[j], odd_conv_b[j], odd_dt_bias[j], odd_a_log[j], odd_d[j], odd_norm_w[j], odd_w_out[j])
            mp, fin = _odd_mixer(hp, *prm, jnp.zeros((bp, N_DIR, H_C, P_C, N_C), f32))
            ms, _ = _odd_mixer(hs, *prm, state_ssd[:, j])
            new_h.append(fin)
        xp = _layer_norm(ALPHA * xp + g1p * mp, ln1_g[l], ln1_b[l])
        xs = _layer_norm(ALPHA * xs + g1s * ms, ln1_g[l], ln1_b[l])
        fp = _peer(xp * (1 + sc2p) + sh2p, peer_w_q[l], peer_sub_keys[l], peer_u[l], peer_v[l])
        fs = _peer(xs * (1 + sc2s) + sh2s, peer_w_q[l], peer_sub_keys[l], peer_u[l], peer_v[l])
        xp = _layer_norm(ALPHA * xp + g2p * fp, ln2_g[l], ln2_b[l])
        xs = _layer_norm(ALPHA * xs + g2s * fs, ln2_g[l], ln2_b[l])
    new_mlstm_C = jnp.stack(new_C, axis=1)
    new_mlstm_n = jnp.stack(new_n, axis=1)
    new_mlstm_m = jnp.stack(new_m, axis=1)
    new_ret = jnp.stack(new_S, axis=1)
    new_ssd = jnp.stack(new_h, axis=1)
    return (xp, xs, new_mlstm_C, new_mlstm_n, new_mlstm_m, new_ret, new_ssd)
```

```python
import functools

import jax
import jax.numpy as jnp
import numpy as np
from jax import lax
from jax.experimental import pallas as pl
from jax.experimental.pallas import tpu as pltpu

D_MODEL = 1024
DEPTH = 4
GRID_W = 64
CHUNK = 64
N_DIR = 2
ALPHA = (2.0 * DEPTH) ** 0.25
H_A = 4
DK_A = 128
DV_A = 128
H_B = 4
DK_B = 128
DV_B = 128
W_A = H_A * DV_A
W_B = H_B * DV_B
EVEN_SIZES = (H_A * DK_A, H_A * DK_A, W_A, W_A, N_DIR * H_A, N_DIR * H_A, H_B * DK_B, H_B * DK_B, W_B, W_B)
EVEN_SPLITS = [int(s) for s in np.cumsum(EVEN_SIZES)[:-1]]
D_INNER = 2 * D_MODEL
P_C = 64
H_C = D_INNER // P_C
G_C = 4
J_C = H_C // G_C
N_C = 128
D_CONV = 5
CONV_DIM = D_INNER + 2 * G_C * N_C
N_KEYS = 128
H_P = 8
D_QK = 256
D_HALF = D_QK // 2
TOPK = 16
PEER_BLOCK = 128

LANE = 128
ROW_TILE = 256


def _mm_kernel(x_ref, w_ref, o_ref):
    o_ref[...] = jnp.dot(x_ref[...].astype(jnp.bfloat16), w_ref[...], preferred_element_type=jnp.float32)


def _pmm(x, w):
    m, k = x.shape
    n = w.shape[1]
    n_pad = -(-n // LANE) * LANE
    wb = jnp.pad(w.astype(jnp.bfloat16), ((0, 0), (0, n_pad - n)))
    tm = ROW_TILE if m % ROW_TILE == 0 else m
    out = pl.pallas_call(
        _mm_kernel,
        grid=(m // tm,),
        in_specs=[pl.BlockSpec((tm, k), lambda i: (i, 0)), pl.BlockSpec((k, n_pad), lambda i: (0, 0))],
        out_specs=pl.BlockSpec((tm, n_pad), lambda i: (i, 0)),
        out_shape=jax.ShapeDtypeStruct((m, n_pad), jnp.float32),
        name="proj_mm",
    )(x, wb)
    return out[:, :n]


def _mm3(x, w):
    b, t, k = x.shape
    return _pmm(x.reshape(b * t, k), w).reshape(b, t, w.shape[1])


def _chunks(a):
    b, t = a.shape[0], a.shape[1]
    return jnp.moveaxis(a.reshape(b, t // CHUNK, CHUNK, *a.shape[2:]), 1, 0)


def _unchunks(a):
    a = jnp.moveaxis(a, 0, 1)
    return a.reshape(a.shape[0], a.shape[1] * a.shape[2], *a.shape[3:])


def _flip(a):
    return jnp.flip(a, axis=1)


def _layer_norm(x, g, b, eps=1e-5):
    xf = x.astype(jnp.float32)
    mu = xf.mean(-1, keepdims=True)
    var = jnp.square(xf - mu).mean(-1, keepdims=True)
    return ((xf - mu) * lax.rsqrt(var + eps) * g + b).astype(x.dtype)


def _head_norm(h, w, eps=1e-5):
    mu = h.mean(-1, keepdims=True)
    var = jnp.square(h - mu).mean(-1, keepdims=True)
    return (h - mu) * lax.rsqrt(var + eps) * w


def _modulation(cvec, w, b):
    m = jax.nn.silu(cvec) @ w + b
    return [t[:, None, :] for t in jnp.split(m, 6, axis=-1)]


def _grid_pos_embed(n_tok, dtype):
    rows = n_tok // GRID_W
    r = jnp.repeat(jnp.arange(rows, dtype=jnp.float32), GRID_W)
    col = jnp.tile(jnp.arange(GRID_W, dtype=jnp.float32), rows)
    quarter = D_MODEL // 4
    freqs = 1.0 / (10000.0 ** (jnp.arange(quarter, dtype=jnp.float32) / quarter))
    er = r[:, None] * freqs
    ec = col[:, None] * freqs
    return jnp.concatenate([jnp.sin(er), jnp.cos(er), jnp.sin(ec), jnp.cos(ec)], axis=-1).astype(dtype)


def _mlstm_scan(q, k, v, li, lf, C0, n0, m0):
    causal = jnp.tril(jnp.ones((CHUNK, CHUNK), dtype=bool))

    def step(carry, inp):
        C, n, m = carry
        qc, kc, vc, lic, lfc = inp
        b = jnp.cumsum(lfc, axis=1)
        dlog = jnp.where(causal[None, :, :, None], b[:, :, None] - b[:, None] + lic[:, None], -jnp.inf)
        inter = b + m[:, None]
        m_t = jnp.maximum(inter, dlog.max(axis=2))
        s = jnp.einsum('bthk,bshk->btsh', qc, kc) * jnp.exp(dlog - m_t[:, :, None])
        w_int = jnp.exp(inter - m_t)
        num = jnp.einsum('btsh,bshv->bthv', s, vc) + w_int[..., None] * jnp.einsum('bthk,bhkv->bthv', qc, C)
        den = s.sum(axis=2) + w_int * jnp.einsum('bthk,bhk->bth', qc, n)
        h = num / jnp.maximum(jnp.abs(den), jnp.exp(-m_t))[..., None]
        g = b[:, -1:] - b + lic
        m_new = jnp.maximum(b[:, -1] + m, g.max(axis=1))
        wg = jnp.exp(g - m_new[:, None])
        dec = jnp.exp(b[:, -1] + m - m_new)
        C_new = dec[..., None, None] * C + jnp.einsum('bsh,bshk,bshv->bhkv', wg, kc, vc)
        n_new = dec[..., None] * n + jnp.einsum('bsh,bshk->bhk', wg, kc)
        return (C_new, n_new, m_new), h

    fin, h = lax.scan(step, (C0, n0, m0), (_chunks(q), _chunks(k), _chunks(v), _chunks(li), _chunks(lf)))
    return _unchunks(h), fin


def _retention_scan(q, k, v, log_gamma, S0):
    idx = jnp.arange(CHUNK, dtype=jnp.float32)
    diff = idx[:, None] - idx[None, :]
    decay_mat = jnp.where((diff >= 0)[..., None], jnp.exp(jnp.maximum(diff, 0.0)[..., None] * log_gamma), 0.0)
    xi = jnp.exp((idx[:, None] + 1.0) * log_gamma)
    zeta = jnp.exp((CHUNK - 1.0 - idx)[:, None] * log_gamma)
    g_end = jnp.exp(CHUNK * log_gamma)

    def step(S, inp):
        qc, kc, vc = inp
        s = jnp.einsum('bthk,bshk->btsh', qc, kc) * decay_mat
        o = jnp.einsum('btsh,bshv->bthv', s, vc) + xi[None, :, :, None] * jnp.einsum('bthk,bhkv->bthv', qc, S)
        S_new = g_end[None, :, None, None] * S + jnp.einsum('bshk,bshv,sh->bhkv', kc, vc, zeta)
        return S_new, o

    fin, o = lax.scan(step, S0, (_chunks(q), _chunks(k), _chunks(v)))
    return _unchunks(o), fin


def _ssd_scan(x, bm, cm, dt, a, h0):
    causal = jnp.tril(jnp.ones((CHUNK, CHUNK), dtype=bool))

    def step(h, inp):
        xc, bc, cc, dtc, ac = inp
        acum = jnp.cumsum(ac, axis=1)
        seg = jnp.where(causal[None, :, :, None, None], acum[:, :, None] - acum[:, None], -jnp.inf)
        s = jnp.einsum('btgn,bsgn->btsg', cc, bc)[..., None] * jnp.exp(seg) * dtc[:, None]
        y = jnp.einsum('btsgj,bsgjp->btgjp', s, xc) + jnp.exp(acum)[..., None] * jnp.einsum('btgn,bgjpn->btgjp', cc, h)
        a_end = acum[:, -1]
        w = jnp.exp(a_end[:, None] - acum) * dtc
        h_new = jnp.exp(a_end)[..., None, None] * h + jnp.einsum('bsgj,bsgjp,bsgn->bgjpn', w, xc, bc)
        return h_new, y

    fin, y = lax.scan(step, h0, (_chunks(x), _chunks(bm), _chunks(cm), _chunks(dt), _chunks(a)))
    return _unchunks(y), fin


def _even_mixer(u, w_in, i_bias, f_bias, log_decay, gn_a, gn_b, w_out, C0, n0, m0, S0):
    f32 = jnp.float32
    bsz, t, _ = u.shape
    qa, ka, va, oa, ia, fa, qb, kb, vb, gb = jnp.split(_mm3(u, w_in), EVEN_SPLITS, axis=-1)
    qa = qa.reshape(bsz, t, H_A, DK_A).astype(f32)
    ka = ka.reshape(bsz, t, H_A, DK_A).astype(f32) * DK_A ** -0.5
    va = va.reshape(bsz, t, H_A, DV_A).astype(f32)
    li = ia.reshape(bsz, t, N_DIR, H_A).astype(f32) + i_bias.astype(f32)
    lf = jax.nn.log_sigmoid(fa.reshape(bsz, t, N_DIR, H_A).astype(f32) + f_bias.astype(f32))
    ha_f, (Cf, nf, mf) = _mlstm_scan(qa, ka, va, li[:, :, 0], lf[:, :, 0], C0[:, 0], n0[:, 0], m0[:, 0])
    ha_b, (Cb, nb, mb) = _mlstm_scan(_flip(qa), _flip(ka), _flip(va), _flip(li[:, :, 1]), _flip(lf[:, :, 1]), C0[:, 1], n0[:, 1], m0[:, 1])
    ha = _head_norm(ha_f + _flip(ha_b), gn_a) * jax.nn.sigmoid(oa.reshape(bsz, t, H_A, DV_A).astype(f32))
    qb = qb.reshape(bsz, t, H_B, DK_B).astype(f32) * DK_B ** -0.5
    kb = kb.reshape(bsz, t, H_B, DK_B).astype(f32)
    vb = vb.reshape(bsz, t, H_B, DV_B).astype(f32)
    lg = -jnp.exp(log_decay.astype(f32))
    hb_f, Sf = _retention_scan(qb, kb, vb, lg[0], S0[:, 0])
    hb_b, Sb = _retention_scan(_flip(qb), _flip(kb), _flip(vb), lg[1], S0[:, 1])
    hb = _head_norm(hb_f + _flip(hb_b), gn_b) * jax.nn.silu(gb.reshape(bsz, t, H_B, DV_B).astype(f32))
    mixed = jnp.concatenate([ha.reshape(bsz, t, W_A), hb.reshape(bsz, t, W_B)], axis=-1).astype(u.dtype)
    fin = (jnp.stack([Cf, Cb], axis=1), jnp.stack([nf, nb], axis=1), jnp.stack([mf, mb], axis=1), jnp.stack([Sf, Sb], axis=1))
    return _mm3(mixed, w_out), fin


def _odd_mixer(u, w_in, conv_w, conv_b, dt_bias, a_log, d_skip, norm_w, w_out, h_st):
    f32 = jnp.float32
    bsz, t, _ = u.shape
    z, xbc, dt = jnp.split(_mm3(u, w_in), [D_INNER, D_INNER + CONV_DIM], axis=-1)
    xbc = lax.conv_general_dilated(xbc, conv_w.astype(xbc.dtype)[:, None, :], window_strides=(1,), padding=[(D_CONV // 2, D_CONV // 2)], dimension_numbers=('NWC', 'WIO', 'NWC'), feature_group_count=CONV_DIM)
    xbc = jax.nn.silu(xbc + conv_b)
    xs, bm, cm = jnp.split(xbc, [D_INNER, D_INNER + G_C * N_C], axis=-1)
    xs = xs.reshape(bsz, t, G_C, J_C, P_C).astype(f32)
    bm = bm.reshape(bsz, t, G_C, N_C).astype(f32)
    cm = cm.reshape(bsz, t, G_C, N_C).astype(f32)
    dt = jax.nn.softplus(dt.reshape(bsz, t, N_DIR, G_C, J_C).astype(f32) + dt_bias.astype(f32).reshape(N_DIR, G_C, J_C))
    A = -jnp.exp(a_log.astype(f32)).reshape(N_DIR, G_C, J_C)
    h0 = h_st.astype(f32).reshape(bsz, N_DIR, G_C, J_C, P_C, N_C)
    y_f, hf = _ssd_scan(xs, bm, cm, dt[:, :, 0], dt[:, :, 0] * A[0], h0[:, 0])
    dt_b = _flip(dt[:, :, 1])
    y_b, hb = _ssd_scan(_flip(xs), _flip(bm), _flip(cm), dt_b, dt_b * A[1], h0[:, 1])
    y = y_f + _flip(y_b) + d_skip.astype(f32).reshape(G_C, J_C, 1) * xs
    y = y.reshape(bsz, t, D_INNER) * jax.nn.silu(z.astype(f32))
    y = y * lax.rsqrt(jnp.square(y).mean(-1, keepdims=True) + 1e-5) * norm_w
    fin = jnp.stack([hf.reshape(bsz, H_C, P_C, N_C), hb.reshape(bsz, H_C, P_C, N_C)], axis=1)
    return _mm3(y.astype(u.dtype), w_out), fin


def _peer(x, w_q, sub_keys, u_tab, v_tab):
    bsz, t, d = x.shape
    qall = _mm3(x, w_q).reshape(-1, PEER_BLOCK, H_P * D_QK)
    xt = x.reshape(-1, PEER_BLOCK, d)

    def block(args):
        xb, qb = args
        q = qb.reshape(PEER_BLOCK, H_P, 2, D_HALF).astype(jnp.float32)
        s = jnp.einsum('thcd,hcnd->thcn', q, sub_keys.astype(jnp.float32))
        sv, si = lax.top_k(s, TOPK)
        cand = (sv[:, :, 0, :, None] + sv[:, :, 1, None, :]).reshape(PEER_BLOCK, H_P, TOPK * TOPK)
        cand_idx = (si[:, :, 0, :, None] * N_KEYS + si[:, :, 1, None, :]).reshape(PEER_BLOCK, H_P, TOPK * TOPK)
        best, pos = lax.top_k(cand, TOPK)
        idx = jnp.take_along_axis(cand_idx, pos, axis=-1)
        g = jax.nn.softmax(best, axis=-1)
        act = jax.nn.gelu(jnp.einsum('thkd,td->thk', u_tab[idx], xb).astype(jnp.float32), approximate=False)
        return jnp.einsum('thk,thkd->td', (g * act).astype(xb.dtype), v_tab[idx])

    return lax.map(block, (xt, qall)).reshape(bsz, t, d)


def kernel(x_prompt, x_sample, c, c_ctx, state_mlstm_C, state_mlstm_n, state_mlstm_m, state_ret, state_ssd, w_mod, b_mod, ln1_g, ln1_b, ln2_g, ln2_b, even_w_in, mlstm_i_bias, mlstm_f_bias, ret_log_decay, even_gn_a, even_gn_b, even_w_out, odd_w_in, odd_conv_w, odd_conv_b, odd_dt_bias, odd_a_log, odd_d, odd_norm_w, odd_w_out, peer_w_q, peer_sub_keys, peer_u, peer_v):
    f32 = jnp.float32
    bp = x_prompt.shape[0]
    xp = x_prompt
    xs = x_sample + _grid_pos_embed(x_sample.shape[1], x_sample.dtype)[None]
    new_C, new_n, new_m, new_S, new_h = [], [], [], [], []
    for l in range(DEPTH):
        sh1p, sc1p, g1p, sh2p, sc2p, g2p = _modulation(c_ctx[None, :], w_mod[l], b_mod[l])
        sh1s, sc1s, g1s, sh2s, sc2s, g2s = _modulation(c, w_mod[l], b_mod[l])
        hp = xp * (1 + sc1p) + sh1p
        hs = xs * (1 + sc1s) + sh1s
        j = l // 2
        if l % 2 == 0:
            prm = (even_w_in[j], mlstm_i_bias[j], mlstm_f_bias[j], ret_log_decay[j], even_gn_a[j], even_gn_b[j], even_w_out[j])
            mp, fin = _even_mixer(hp, *prm, jnp.zeros((bp, N_DIR, H_A, DK_A, DV_A), f32), jnp.zeros((bp, N_DIR, H_A, DK_A), f32), jnp.zeros((bp, N_DIR, H_A), f32), jnp.zeros((bp, N_DIR, H_B, DK_B, DV_B), f32))
            ms, _ = _even_mixer(hs, *prm, state_mlstm_C[:, j], state_mlstm_n[:, j], state_mlstm_m[:, j], state_ret[:, j])
            new_C.append(fin[0])
            new_n.append(fin[1])
            new_m.append(fin[2])
            new_S.append(fin[3])
        else:
            prm = (odd_w_in[j], odd_conv_w[j], odd_conv_b[j], odd_dt_bias[j], odd_a_log[j], odd_d[j], odd_norm_w[j], odd_w_out[j])
            mp, fin = _odd_mixer(hp, *prm, jnp.zeros((bp, N_DIR, H_C, P_C, N_C), f32))
            ms, _ = _odd_mixer(hs, *prm, state_ssd[:, j])
            new_h.append(fin)
        xp = _layer_norm(ALPHA * xp + g1p * mp, ln1_g[l], ln1_b[l])
        xs = _layer_norm(ALPHA * xs + g1s * ms, ln1_g[l], ln1_b[l])
        fp = _peer(xp * (1 + sc2p) + sh2p, peer_w_q[l], peer_sub_keys[l], peer_u[l], peer_v[l])
        fs = _peer(xs * (1 + sc2s) + sh2s, peer_w_q[l], peer_sub_keys[l], peer_u[l], peer_v[l])
        xp = _layer_norm(ALPHA * xp + g2p * fp, ln2_g[l], ln2_b[l])
        xs = _layer_norm(ALPHA * xs + g2s * fs, ln2_g[l], ln2_b[l])
    return (xp, xs, jnp.stack(new_C, axis=1), jnp.stack(new_n, axis=1), jnp.stack(new_m, axis=1), jnp.stack(new_S, axis=1), jnp.stack(new_h, axis=1))
```

```python
import functools

import jax
import jax.numpy as jnp
import numpy as np
from jax import lax
from jax.experimental import pallas as pl
from jax.experimental.pallas import tpu as pltpu

D_MODEL = 1024
DEPTH = 4
GRID_W = 64
CHUNK = 64
N_DIR = 2
ALPHA = (2.0 * DEPTH) ** 0.25
H_A = 4
DK_A = 128
DV_A = 128
H_B = 4
DK_B = 128
DV_B = 128
W_A = H_A * DV_A
W_B = H_B * DV_B
EVEN_SIZES = (H_A * DK_A, H_A * DK_A, W_A, W_A, N_DIR * H_A, N_DIR * H_A, H_B * DK_B, H_B * DK_B, W_B, W_B)
EVEN_SPLITS = [int(s) for s in np.cumsum(EVEN_SIZES)[:-1]]
D_INNER = 2 * D_MODEL
P_C = 64
H_C = D_INNER // P_C
G_C = 4
J_C = H_C // G_C
N_C = 128
D_CONV = 5
CONV_DIM = D_INNER + 2 * G_C * N_C
N_KEYS = 128
H_P = 8
D_QK = 256
D_HALF = D_QK // 2
TOPK = 16
PEER_BLOCK = 128

LANE = 128
ROW_TILE = 256
PEER_PAIRS = H_P * TOPK
PEER_TB = 128
PEER_NBUF = 8
SLAB_ROWS = 2 * D_MODEL // LANE
SLAB_PITCH = SLAB_ROWS + 4


def _mm_kernel(x_ref, w_ref, o_ref):
    o_ref[...] = jnp.dot(x_ref[...].astype(jnp.bfloat16), w_ref[...], preferred_element_type=jnp.float32)


def _pmm(x, w):
    m, k = x.shape
    n = w.shape[1]
    n_pad = -(-n // LANE) * LANE
    wb = jnp.pad(w.astype(jnp.bfloat16), ((0, 0), (0, n_pad - n)))
    tm = ROW_TILE if m % ROW_TILE == 0 else m
    out = pl.pallas_call(
        _mm_kernel,
        grid=(m // tm,),
        in_specs=[pl.BlockSpec((tm, k), lambda i: (i, 0)), pl.BlockSpec((k, n_pad), lambda i: (0, 0))],
        out_specs=pl.BlockSpec((tm, n_pad), lambda i: (i, 0)),
        out_shape=jax.ShapeDtypeStruct((m, n_pad), jnp.float32),
        name="proj_mm",
    )(x, wb)
    return out[:, :n]


def _mm3(x, w):
    b, t, k = x.shape
    return _pmm(x.reshape(b * t, k), w).reshape(b, t, w.shape[1])


def _chunks(a):
    b, t = a.shape[0], a.shape[1]
    return jnp.moveaxis(a.reshape(b, t // CHUNK, CHUNK, *a.shape[2:]), 1, 0)


def _unchunks(a):
    a = jnp.moveaxis(a, 0, 1)
    return a.reshape(a.shape[0], a.shape[1] * a.shape[2], *a.shape[3:])


def _flip(a):
    return jnp.flip(a, axis=1)


def _layer_norm(x, g, b, eps=1e-5):
    xf = x.astype(jnp.float32)
    mu = xf.mean(-1, keepdims=True)
    var = jnp.square(xf - mu).mean(-1, keepdims=True)
    return ((xf - mu) * lax.rsqrt(var + eps) * g + b).astype(x.dtype)


def _head_norm(h, w, eps=1e-5):
    mu = h.mean(-1, keepdims=True)
    var = jnp.square(h - mu).mean(-1, keepdims=True)
    return (h - mu) * lax.rsqrt(var + eps) * w


def _modulation(cvec, w, b):
    m = jax.nn.silu(cvec) @ w + b
    return [t[:, None, :] for t in jnp.split(m, 6, axis=-1)]


def _grid_pos_embed(n_tok, dtype):
    rows = n_tok // GRID_W
    r = jnp.repeat(jnp.arange(rows, dtype=jnp.float32), GRID_W)
    col = jnp.tile(jnp.arange(GRID_W, dtype=jnp.float32), rows)
    quarter = D_MODEL // 4
    freqs = 1.0 / (10000.0 ** (jnp.arange(quarter, dtype=jnp.float32) / quarter))
    er = r[:, None] * freqs
    ec = col[:, None] * freqs
    return jnp.concatenate([jnp.sin(er), jnp.cos(er), jnp.sin(ec), jnp.cos(ec)], axis=-1).astype(dtype)


def _mlstm_scan(q, k, v, li, lf, C0, n0, m0):
    causal = jnp.tril(jnp.ones((CHUNK, CHUNK), dtype=bool))

    def step(carry, inp):
        C, n, m = carry
        qc, kc, vc, lic, lfc = inp
        b = jnp.cumsum(lfc, axis=1)
        dlog = jnp.where(causal[None, :, :, None], b[:, :, None] - b[:, None] + lic[:, None], -jnp.inf)
        inter = b + m[:, None]
        m_t = jnp.maximum(inter, dlog.max(axis=2))
        s = jnp.einsum('bthk,bshk->btsh', qc, kc) * jnp.exp(dlog - m_t[:, :, None])
        w_int = jnp.exp(inter - m_t)
        num = jnp.einsum('btsh,bshv->bthv', s, vc) + w_int[..., None] * jnp.einsum('bthk,bhkv->bthv', qc, C)
        den = s.sum(axis=2) + w_int * jnp.einsum('bthk,bhk->bth', qc, n)
        h = num / jnp.maximum(jnp.abs(den), jnp.exp(-m_t))[..., None]
        g = b[:, -1:] - b + lic
        m_new = jnp.maximum(b[:, -1] + m, g.max(axis=1))
        wg = jnp.exp(g - m_new[:, None])
        dec = jnp.exp(b[:, -1] + m - m_new)
        C_new = dec[..., None, None] * C + jnp.einsum('bsh,bshk,bshv->bhkv', wg, kc, vc)
        n_new = dec[..., None] * n + jnp.einsum('bsh,bshk->bhk', wg, kc)
        return (C_new, n_new, m_new), h

    fin, h = lax.scan(step, (C0, n0, m0), (_chunks(q), _chunks(k), _chunks(v), _chunks(li), _chunks(lf)))
    return _unchunks(h), fin


def _retention_scan(q, k, v, log_gamma, S0):
    idx = jnp.arange(CHUNK, dtype=jnp.float32)
    diff = idx[:, None] - idx[None, :]
    decay_mat = jnp.where((diff >= 0)[..., None], jnp.exp(jnp.maximum(diff, 0.0)[..., None] * log_gamma), 0.0)
    xi = jnp.exp((idx[:, None] + 1.0) * log_gamma)
    zeta = jnp.exp((CHUNK - 1.0 - idx)[:, None] * log_gamma)
    g_end = jnp.exp(CHUNK * log_gamma)

    def step(S, inp):
        qc, kc, vc = inp
        s = jnp.einsum('bthk,bshk->btsh', qc, kc) * decay_mat
        o = jnp.einsum('btsh,bshv->bthv', s, vc) + xi[None, :, :, None] * jnp.einsum('bthk,bhkv->bthv', qc, S)
        S_new = g_end[None, :, None, None] * S + jnp.einsum('bshk,bshv,sh->bhkv', kc, vc, zeta)
        return S_new, o

    fin, o = lax.scan(step, S0, (_chunks(q), _chunks(k), _chunks(v)))
    return _unchunks(o), fin


def _ssd_scan(x, bm, cm, dt, a, h0):
    causal = jnp.tril(jnp.ones((CHUNK, CHUNK), dtype=bool))

    def step(h, inp):
        xc, bc, cc, dtc, ac = inp
        acum = jnp.cumsum(ac, axis=1)
        seg = jnp.where(causal[None, :, :, None, None], acum[:, :, None] - acum[:, None], -jnp.inf)
        s = jnp.einsum('btgn,bsgn->btsg', cc, bc)[..., None] * jnp.exp(seg) * dtc[:, None]
        y = jnp.einsum('btsgj,bsgjp->btgjp', s, xc) + jnp.exp(acum)[..., None] * jnp.einsum('btgn,bgjpn->btgjp', cc, h)
        a_end = acum[:, -1]
        w = jnp.exp(a_end[:, None] - acum) * dtc
        h_new = jnp.exp(a_end)[..., None, None] * h + jnp.einsum('bsgj,bsgjp,bsgn->bgjpn', w, xc, bc)
        return h_new, y

    fin, y = lax.scan(step, h0, (_chunks(x), _chunks(bm), _chunks(cm), _chunks(dt), _chunks(a)))
    return _unchunks(y), fin


def _even_mixer(u, w_in, i_bias, f_bias, log_decay, gn_a, gn_b, w_out, C0, n0, m0, S0):
    f32 = jnp.float32
    bsz, t, _ = u.shape
    qa, ka, va, oa, ia, fa, qb, kb, vb, gb = jnp.split(_mm3(u, w_in), EVEN_SPLITS, axis=-1)
    qa = qa.reshape(bsz, t, H_A, DK_A).astype(f32)
    ka = ka.reshape(bsz, t, H_A, DK_A).astype(f32) * DK_A ** -0.5
    va = va.reshape(bsz, t, H_A, DV_A).astype(f32)
    li = ia.reshape(bsz, t, N_DIR, H_A).astype(f32) + i_bias.astype(f32)
    lf = jax.nn.log_sigmoid(fa.reshape(bsz, t, N_DIR, H_A).astype(f32) + f_bias.astype(f32))
    ha_f, (Cf, nf, mf) = _mlstm_scan(qa, ka, va, li[:, :, 0], lf[:, :, 0], C0[:, 0], n0[:, 0], m0[:, 0])
    ha_b, (Cb, nb, mb) = _mlstm_scan(_flip(qa), _flip(ka), _flip(va), _flip(li[:, :, 1]), _flip(lf[:, :, 1]), C0[:, 1], n0[:, 1], m0[:, 1])
    ha = _head_norm(ha_f + _flip(ha_b), gn_a) * jax.nn.sigmoid(oa.reshape(bsz, t, H_A, DV_A).astype(f32))
    qb = qb.reshape(bsz, t, H_B, DK_B).astype(f32) * DK_B ** -0.5
    kb = kb.reshape(bsz, t, H_B, DK_B).astype(f32)
    vb = vb.reshape(bsz, t, H_B, DV_B).astype(f32)
    lg = -jnp.exp(log_decay.astype(f32))
    hb_f, Sf = _retention_scan(qb, kb, vb, lg[0], S0[:, 0])
    hb_b, Sb = _retention_scan(_flip(qb), _flip(kb), _flip(vb), lg[1], S0[:, 1])
    hb = _head_norm(hb_f + _flip(hb_b), gn_b) * jax.nn.silu(gb.reshape(bsz, t, H_B, DV_B).astype(f32))
    mixed = jnp.concatenate([ha.reshape(bsz, t, W_A), hb.reshape(bsz, t, W_B)], axis=-1).astype(u.dtype)
    fin = (jnp.stack([Cf, Cb], axis=1), jnp.stack([nf, nb], axis=1), jnp.stack([mf, mb], axis=1), jnp.stack([Sf, Sb], axis=1))
    return _mm3(mixed, w_out), fin


def _odd_mixer(u, w_in, conv_w, conv_b, dt_bias, a_log, d_skip, norm_w, w_out, h_st):
    f32 = jnp.float32
    bsz, t, _ = u.shape
    z, xbc, dt = jnp.split(_mm3(u, w_in), [D_INNER, D_INNER + CONV_DIM], axis=-1)
    xbc = lax.conv_general_dilated(xbc, conv_w.astype(xbc.dtype)[:, None, :], window_strides=(1,), padding=[(D_CONV // 2, D_CONV // 2)], dimension_numbers=('NWC', 'WIO', 'NWC'), feature_group_count=CONV_DIM)
    xbc = jax.nn.silu(xbc + conv_b)
    xs, bm, cm = jnp.split(xbc, [D_INNER, D_INNER + G_C * N_C], axis=-1)
    xs = xs.reshape(bsz, t, G_C, J_C, P_C).astype(f32)
    bm = bm.reshape(bsz, t, G_C, N_C).astype(f32)
    cm = cm.reshape(bsz, t, G_C, N_C).astype(f32)
    dt = jax.nn.softplus(dt.reshape(bsz, t, N_DIR, G_C, J_C).astype(f32) + dt_bias.astype(f32).reshape(N_DIR, G_C, J_C))
    A = -jnp.exp(a_log.astype(f32)).reshape(N_DIR, G_C, J_C)
    h0 = h_st.astype(f32).reshape(bsz, N_DIR, G_C, J_C, P_C, N_C)
    y_f, hf = _ssd_scan(xs, bm, cm, dt[:, :, 0], dt[:, :, 0] * A[0], h0[:, 0])
    dt_b = _flip(dt[:, :, 1])
    y_b, hb = _ssd_scan(_flip(xs), _flip(bm), _flip(cm), dt_b, dt_b * A[1], h0[:, 1])
    y = y_f + _flip(y_b) + d_skip.astype(f32).reshape(G_C, J_C, 1) * xs
    y = y.reshape(bsz, t, D_INNER) * jax.nn.silu(z.astype(f32))
    y = y * lax.rsqrt(jnp.square(y).mean(-1, keepdims=True) + 1e-5) * norm_w
    fin = jnp.stack([hf.reshape(bsz, H_C, P_C, N_C), hb.reshape(bsz, H_C, P_C, N_C)], axis=1)
    return _mm3(y.astype(u.dtype), w_out), fin


def _expert_slab_copy(tab_ref, buf_ref, sem_ref, row, pair, slot):
    src = tab_ref.at[pl.ds(pl.multiple_of(row, SLAB_ROWS), SLAB_ROWS), :]
    return pltpu.make_async_copy(src, buf_ref.at[slot, pl.ds(pair * SLAB_PITCH, SLAB_ROWS), :], sem_ref.at[slot])


def _peer_gather_kernel(idx_ref, x_ref, g_ref, tab_ref, o_ref, buf_ref, sem_ref):
    n_tok = x_ref.shape[0]
    n_chunk = x_ref.shape[1] // LANE

    per_phase = PEER_PAIRS // (2 * n_chunk)

    def issue(t, slot, p0, p1):
        for p in range(p0, p1):
            _expert_slab_copy(tab_ref, buf_ref, sem_ref, idx_ref[t, p], p, slot).start(priority=p % 2)

    def wait(slot):
        n_rows = PEER_PAIRS * SLAB_ROWS
        pltpu.make_async_copy(tab_ref.at[pl.ds(0, n_rows), :], buf_ref.at[slot, pl.ds(0, n_rows), :], sem_ref.at[slot]).wait()

    def chunk(slot, c):
        return buf_ref[slot, pl.ds(c, PEER_PAIRS, stride=SLAB_PITCH), :]

    eye = lax.broadcasted_iota(jnp.int32, (PEER_PAIRS, PEER_PAIRS), 0) == lax.broadcasted_iota(jnp.int32, (PEER_PAIRS, PEER_PAIRS), 1)

    def compute(t, slot, t_next, slot_next):
        phase = [0]

        def issue_some():
            if t_next is not None:
                issue(t_next, slot_next, phase[0] * per_phase, (phase[0] + 1) * per_phase)
            phase[0] += 1

        wait(slot)
        gt = g_ref[pl.ds(t, 1), :]
        xt = x_ref[pl.ds(t, 1), :]
        acc = chunk(slot, 0) * xt[:, :LANE]
        issue_some()
        for c in range(1, n_chunk):
            acc = acc + chunk(slot, c) * xt[:, c * LANE:(c + 1) * LANE]
            issue_some()
        act = jnp.sum(acc, axis=1, keepdims=True)
        g_col = jnp.sum(jnp.where(eye, gt, 0.0), axis=1, keepdims=True)
        w = g_col * (0.5 * act * (1.0 + lax.erf(act * np.float32(np.sqrt(0.5)))))
        outs = []
        for c in range(n_chunk):
            outs.append(jnp.sum(w * chunk(slot, n_chunk + c), axis=0, keepdims=True))
            issue_some()
        o_ref[pl.ds(t, 1), :] = jnp.concatenate(outs, axis=1)

    ahead = PEER_NBUF - 1
    for s in range(ahead):
        issue(s, s, 0, PEER_PAIRS)

    def group(base, n_issue):
        for j in range(PEER_NBUF):
            if j < n_issue:
                compute(base + j, j, base + j + ahead, (j + ahead) % PEER_NBUF)
            else:
                compute(base + j, j, None, None)

    n_group = n_tok // PEER_NBUF

    def steady(gi, carry):
        group(gi * PEER_NBUF, PEER_NBUF)
        return carry

    lax.fori_loop(0, n_group - 1, steady, 0)
    group((n_group - 1) * PEER_NBUF, PEER_NBUF - ahead)


def _peer_gather(x, idx, g, uv_tab):
    n, d = x.shape
    return pl.pallas_call(
        _peer_gather_kernel,
        grid=(n // PEER_TB,),
        in_specs=[
            pl.BlockSpec((PEER_TB, PEER_PAIRS), lambda i: (i, 0), memory_space=pltpu.SMEM),
            pl.BlockSpec((PEER_TB, d), lambda i: (i, 0)),
            pl.BlockSpec((PEER_TB, PEER_PAIRS), lambda i: (i, 0)),
            pl.BlockSpec(memory_space=pl.ANY),
        ],
        out_specs=pl.BlockSpec((PEER_TB, d), lambda i: (i, 0)),
        out_shape=jax.ShapeDtypeStruct((n, d), jnp.float32),
        scratch_shapes=[pltpu.VMEM((PEER_NBUF, PEER_PAIRS * SLAB_PITCH, LANE), jnp.float32), pltpu.SemaphoreType.DMA((PEER_NBUF,))],
        name="peer_gather",
    )(idx * SLAB_ROWS, x, g, uv_tab)


def _peer(x, w_q, sub_keys, uv_tab):
    bsz, t, d = x.shape
    n = bsz * t
    q = _mm3(x, w_q).reshape(n, H_P, 2, D_HALF).astype(jnp.float32)
    s = jnp.einsum('thcd,hcnd->thcn', q, sub_keys.astype(jnp.float32))
    sv, si = lax.top_k(s, TOPK)
    cand = (sv[:, :, 0, :, None] + sv[:, :, 1, None, :]).reshape(n, H_P, TOPK * TOPK)
    cand_idx = (si[:, :, 0, :, None] * N_KEYS + si[:, :, 1, None, :]).reshape(n, H_P, TOPK * TOPK)
    best, pos = lax.top_k(cand, TOPK)
    idx = jnp.take_along_axis(cand_idx, pos, axis=-1)
    g = jax.nn.softmax(best, axis=-1)
    out = _peer_gather(x.reshape(n, d), idx.reshape(n, PEER_PAIRS).astype(jnp.int32), g.reshape(n, PEER_PAIRS), uv_tab)
    return out.reshape(bsz, t, d)


def kernel(x_prompt, x_sample, c, c_ctx, state_mlstm_C, state_mlstm_n, state_mlstm_m, state_ret, state_ssd, w_mod, b_mod, ln1_g, ln1_b, ln2_g, ln2_b, even_w_in, mlstm_i_bias, mlstm_f_bias, ret_log_decay, even_gn_a, even_gn_b, even_w_out, odd_w_in, odd_conv_w, odd_conv_b, odd_dt_bias, odd_a_log, odd_d, odd_norm_w, odd_w_out, peer_w_q, peer_sub_keys, peer_u, peer_v):
    f32 = jnp.float32
    bp = x_prompt.shape[0]
    xp = x_prompt
    xs = x_sample + _grid_pos_embed(x_sample.shape[1], x_sample.dtype)[None]
    new_C, new_n, new_m, new_S, new_h = [], [], [], [], []
    for l in range(DEPTH):
        sh1p, sc1p, g1p, sh2p, sc2p, g2p = _modulation(c_ctx[None, :], w_mod[l], b_mod[l])
        sh1s, sc1s, g1s, sh2s, sc2s, g2s = _modulation(c, w_mod[l], b_mod[l])
        hp = xp * (1 + sc1p) + sh1p
        hs = xs * (1 + sc1s) + sh1s
        j = l // 2
        if l % 2 == 0:
            prm = (even_w_in[j], mlstm_i_bias[j], mlstm_f_bias[j], ret_log_decay[j], even_gn_a[j], even_gn_b[j], even_w_out[j])
            mp, fin = _even_mixer(hp, *prm, jnp.zeros((bp, N_DIR, H_A, DK_A, DV_A), f32), jnp.zeros((bp, N_DIR, H_A, DK_A), f32), jnp.zeros((bp, N_DIR, H_A), f32), jnp.zeros((bp, N_DIR, H_B, DK_B, DV_B), f32))
            ms, _ = _even_mixer(hs, *prm, state_mlstm_C[:, j], state_mlstm_n[:, j], state_mlstm_m[:, j], state_ret[:, j])
            new_C.append(fin[0])
            new_n.append(fin[1])
            new_m.append(fin[2])
            new_S.append(fin[3])
        else:
            prm = (odd_w_in[j], odd_conv_w[j], odd_conv_b[j], odd_dt_bias[j], odd_a_log[j], odd_d[j], odd_norm_w[j], odd_w_out[j])
            mp, fin = _odd_mixer(hp, *prm, jnp.zeros((bp, N_DIR, H_C, P_C, N_C), f32))
            ms, _ = _odd_mixer(hs, *prm, state_ssd[:, j])
            new_h.append(fin)
        xp = _layer_norm(ALPHA * xp + g1p * mp, ln1_g[l], ln1_b[l])
        xs = _layer_norm(ALPHA * xs + g1s * ms, ln1_g[l], ln1_b[l])
        uv_tab = jnp.concatenate([peer_u[l], peer_v[l]], axis=1).reshape(-1, LANE)
        fp = _peer(xp * (1 + sc2p) + sh2p, peer_w_q[l], peer_sub_keys[l], uv_tab)
        fs = _peer(xs * (1 + sc2s) + sh2s, peer_w_q[l], peer_sub_keys[l], uv_tab)
        xp = _layer_norm(ALPHA * xp + g2p * fp, ln2_g[l], ln2_b[l])
        xs = _layer_norm(ALPHA * xs + g2s * fs, ln2_g[l], ln2_b[l])
    return (xp, xs, jnp.stack(new_C, axis=1), jnp.stack(new_n, axis=1), jnp.stack(new_m, axis=1), jnp.stack(new_S, axis=1), jnp.stack(new_h, axis=1))
```

```python
import functools

import jax
import jax.numpy as jnp
import numpy as np
from jax import lax
from jax.experimental import pallas as pl
from jax.experimental.pallas import tpu as pltpu

D_MODEL = 1024
DEPTH = 4
GRID_W = 64
CHUNK = 64
N_DIR = 2
ALPHA = (2.0 * DEPTH) ** 0.25
H_A = 4
DK_A = 128
DV_A = 128
H_B = 4
DK_B = 128
DV_B = 128
W_A = H_A * DV_A
W_B = H_B * DV_B
EVEN_SIZES = (H_A * DK_A, H_A * DK_A, W_A, W_A, N_DIR * H_A, N_DIR * H_A, H_B * DK_B, H_B * DK_B, W_B, W_B)
EVEN_SPLITS = [int(s) for s in np.cumsum(EVEN_SIZES)[:-1]]
D_INNER = 2 * D_MODEL
P_C = 64
H_C = D_INNER // P_C
G_C = 4
J_C = H_C // G_C
N_C = 128
D_CONV = 5
CONV_DIM = D_INNER + 2 * G_C * N_C
N_KEYS = 128
H_P = 8
D_QK = 256
D_HALF = D_QK // 2
TOPK = 16
PEER_BLOCK = 128

LANE = 128
ROW_TILE = 256
PEER_PAIRS = H_P * TOPK
SEL_TM = 256
PEER_TB = 128
PEER_NBUF = 8
SLAB_ROWS = 2 * D_MODEL // LANE
SLAB_PITCH = SLAB_ROWS + 4


def _mm_kernel(x_ref, w_ref, o_ref):
    o_ref[...] = jnp.dot(x_ref[...].astype(jnp.bfloat16), w_ref[...], preferred_element_type=jnp.float32)


def _pmm(x, w):
    m, k = x.shape
    n = w.shape[1]
    n_pad = -(-n // LANE) * LANE
    wb = jnp.pad(w.astype(jnp.bfloat16), ((0, 0), (0, n_pad - n)))
    tm = ROW_TILE if m % ROW_TILE == 0 else m
    out = pl.pallas_call(
        _mm_kernel,
        grid=(m // tm,),
        in_specs=[pl.BlockSpec((tm, k), lambda i: (i, 0)), pl.BlockSpec((k, n_pad), lambda i: (0, 0))],
        out_specs=pl.BlockSpec((tm, n_pad), lambda i: (i, 0)),
        out_shape=jax.ShapeDtypeStruct((m, n_pad), jnp.float32),
        name="proj_mm",
    )(x, wb)
    return out[:, :n]


def _mm3(x, w):
    b, t, k = x.shape
    return _pmm(x.reshape(b * t, k), w).reshape(b, t, w.shape[1])


def _chunks(a):
    b, t = a.shape[0], a.shape[1]
    return jnp.moveaxis(a.reshape(b, t // CHUNK, CHUNK, *a.shape[2:]), 1, 0)


def _unchunks(a):
    a = jnp.moveaxis(a, 0, 1)
    return a.reshape(a.shape[0], a.shape[1] * a.shape[2], *a.shape[3:])


def _flip(a):
    return jnp.flip(a, axis=1)


def _layer_norm(x, g, b, eps=1e-5):
    xf = x.astype(jnp.float32)
    mu = xf.mean(-1, keepdims=True)
    var = jnp.square(xf - mu).mean(-1, keepdims=True)
    return ((xf - mu) * lax.rsqrt(var + eps) * g + b).astype(x.dtype)


def _head_norm(h, w, eps=1e-5):
    mu = h.mean(-1, keepdims=True)
    var = jnp.square(h - mu).mean(-1, keepdims=True)
    return (h - mu) * lax.rsqrt(var + eps) * w


def _modulation(cvec, w, b):
    m = jax.nn.silu(cvec) @ w + b
    return [t[:, None, :] for t in jnp.split(m, 6, axis=-1)]


def _grid_pos_embed(n_tok, dtype):
    rows = n_tok // GRID_W
    r = jnp.repeat(jnp.arange(rows, dtype=jnp.float32), GRID_W)
    col = jnp.tile(jnp.arange(GRID_W, dtype=jnp.float32), rows)
    quarter = D_MODEL // 4
    freqs = 1.0 / (10000.0 ** (jnp.arange(quarter, dtype=jnp.float32) / quarter))
    er = r[:, None] * freqs
    ec = col[:, None] * freqs
    return jnp.concatenate([jnp.sin(er), jnp.cos(er), jnp.sin(ec), jnp.cos(ec)], axis=-1).astype(dtype)


def _mlstm_scan(q, k, v, li, lf, C0, n0, m0):
    causal = jnp.tril(jnp.ones((CHUNK, CHUNK), dtype=bool))

    def step(carry, inp):
        C, n, m = carry
        qc, kc, vc, lic, lfc = inp
        b = jnp.cumsum(lfc, axis=1)
        dlog = jnp.where(causal[None, :, :, None], b[:, :, None] - b[:, None] + lic[:, None], -jnp.inf)
        inter = b + m[:, None]
        m_t = jnp.maximum(inter, dlog.max(axis=2))
        s = jnp.einsum('bthk,bshk->btsh', qc, kc) * jnp.exp(dlog - m_t[:, :, None])
        w_int = jnp.exp(inter - m_t)
        num = jnp.einsum('btsh,bshv->bthv', s, vc) + w_int[..., None] * jnp.einsum('bthk,bhkv->bthv', qc, C)
        den = s.sum(axis=2) + w_int * jnp.einsum('bthk,bhk->bth', qc, n)
        h = num / jnp.maximum(jnp.abs(den), jnp.exp(-m_t))[..., None]
        g = b[:, -1:] - b + lic
        m_new = jnp.maximum(b[:, -1] + m, g.max(axis=1))
        wg = jnp.exp(g - m_new[:, None])
        dec = jnp.exp(b[:, -1] + m - m_new)
        C_new = dec[..., None, None] * C + jnp.einsum('bsh,bshk,bshv->bhkv', wg, kc, vc)
        n_new = dec[..., None] * n + jnp.einsum('bsh,bshk->bhk', wg, kc)
        return (C_new, n_new, m_new), h

    fin, h = lax.scan(step, (C0, n0, m0), (_chunks(q), _chunks(k), _chunks(v), _chunks(li), _chunks(lf)))
    return _unchunks(h), fin


def _retention_scan(q, k, v, log_gamma, S0):
    idx = jnp.arange(CHUNK, dtype=jnp.float32)
    diff = idx[:, None] - idx[None, :]
    decay_mat = jnp.where((diff >= 0)[..., None], jnp.exp(jnp.maximum(diff, 0.0)[..., None] * log_gamma), 0.0)
    xi = jnp.exp((idx[:, None] + 1.0) * log_gamma)
    zeta = jnp.exp((CHUNK - 1.0 - idx)[:, None] * log_gamma)
    g_end = jnp.exp(CHUNK * log_gamma)

    def step(S, inp):
        qc, kc, vc = inp
        s = jnp.einsum('bthk,bshk->btsh', qc, kc) * decay_mat
        o = jnp.einsum('btsh,bshv->bthv', s, vc) + xi[None, :, :, None] * jnp.einsum('bthk,bhkv->bthv', qc, S)
        S_new = g_end[None, :, None, None] * S + jnp.einsum('bshk,bshv,sh->bhkv', kc, vc, zeta)
        return S_new, o

    fin, o = lax.scan(step, S0, (_chunks(q), _chunks(k), _chunks(v)))
    return _unchunks(o), fin


def _ssd_scan(x, bm, cm, dt, a, h0):
    causal = jnp.tril(jnp.ones((CHUNK, CHUNK), dtype=bool))

    def step(h, inp):
        xc, bc, cc, dtc, ac = inp
        acum = jnp.cumsum(ac, axis=1)
        seg = jnp.where(causal[None, :, :, None, None], acum[:, :, None] - acum[:, None], -jnp.inf)
        s = jnp.einsum('btgn,bsgn->btsg', cc, bc)[..., None] * jnp.exp(seg) * dtc[:, None]
        y = jnp.einsum('btsgj,bsgjp->btgjp', s, xc) + jnp.exp(acum)[..., None] * jnp.einsum('btgn,bgjpn->btgjp', cc, h)
        a_end = acum[:, -1]
        w = jnp.exp(a_end[:, None] - acum) * dtc
        h_new = jnp.exp(a_end)[..., None, None] * h + jnp.einsum('bsgj,bsgjp,bsgn->bgjpn', w, xc, bc)
        return h_new, y

    fin, y = lax.scan(step, h0, (_chunks(x), _chunks(bm), _chunks(cm), _chunks(dt), _chunks(a)))
    return _unchunks(y), fin


def _even_mixer(u, w_in, i_bias, f_bias, log_decay, gn_a, gn_b, w_out, C0, n0, m0, S0):
    f32 = jnp.float32
    bsz, t, _ = u.shape
    qa, ka, va, oa, ia, fa, qb, kb, vb, gb = jnp.split(_mm3(u, w_in), EVEN_SPLITS, axis=-1)
    qa = qa.reshape(bsz, t, H_A, DK_A).astype(f32)
    ka = ka.reshape(bsz, t, H_A, DK_A).astype(f32) * DK_A ** -0.5
    va = va.reshape(bsz, t, H_A, DV_A).astype(f32)
    li = ia.reshape(bsz, t, N_DIR, H_A).astype(f32) + i_bias.astype(f32)
    lf = jax.nn.log_sigmoid(fa.reshape(bsz, t, N_DIR, H_A).astype(f32) + f_bias.astype(f32))
    ha_f, (Cf, nf, mf) = _mlstm_scan(qa, ka, va, li[:, :, 0], lf[:, :, 0], C0[:, 0], n0[:, 0], m0[:, 0])
    ha_b, (Cb, nb, mb) = _mlstm_scan(_flip(qa), _flip(ka), _flip(va), _flip(li[:, :, 1]), _flip(lf[:, :, 1]), C0[:, 1], n0[:, 1], m0[:, 1])
    ha = _head_norm(ha_f + _flip(ha_b), gn_a) * jax.nn.sigmoid(oa.reshape(bsz, t, H_A, DV_A).astype(f32))
    qb = qb.reshape(bsz, t, H_B, DK_B).astype(f32) * DK_B ** -0.5
    kb = kb.reshape(bsz, t, H_B, DK_B).astype(f32)
    vb = vb.reshape(bsz, t, H_B, DV_B).astype(f32)
    lg = -jnp.exp(log_decay.astype(f32))
    hb_f, Sf = _retention_scan(qb, kb, vb, lg[0], S0[:, 0])
    hb_b, Sb = _retention_scan(_flip(qb), _flip(kb), _flip(vb), lg[1], S0[:, 1])
    hb = _head_norm(hb_f + _flip(hb_b), gn_b) * jax.nn.silu(gb.reshape(bsz, t, H_B, DV_B).astype(f32))
    mixed = jnp.concatenate([ha.reshape(bsz, t, W_A), hb.reshape(bsz, t, W_B)], axis=-1).astype(u.dtype)
    fin = (jnp.stack([Cf, Cb], axis=1), jnp.stack([nf, nb], axis=1), jnp.stack([mf, mb], axis=1), jnp.stack([Sf, Sb], axis=1))
    return _mm3(mixed, w_out), fin


def _odd_mixer(u, w_in, conv_w, conv_b, dt_bias, a_log, d_skip, norm_w, w_out, h_st):
    f32 = jnp.float32
    bsz, t, _ = u.shape
    z, xbc, dt = jnp.split(_mm3(u, w_in), [D_INNER, D_INNER + CONV_DIM], axis=-1)
    xbc = lax.conv_general_dilated(xbc, conv_w.astype(xbc.dtype)[:, None, :], window_strides=(1,), padding=[(D_CONV // 2, D_CONV // 2)], dimension_numbers=('NWC', 'WIO', 'NWC'), feature_group_count=CONV_DIM)
    xbc = jax.nn.silu(xbc + conv_b)
    xs, bm, cm = jnp.split(xbc, [D_INNER, D_INNER + G_C * N_C], axis=-1)
    xs = xs.reshape(bsz, t, G_C, J_C, P_C).astype(f32)
    bm = bm.reshape(bsz, t, G_C, N_C).astype(f32)
    cm = cm.reshape(bsz, t, G_C, N_C).astype(f32)
    dt = jax.nn.softplus(dt.reshape(bsz, t, N_DIR, G_C, J_C).astype(f32) + dt_bias.astype(f32).reshape(N_DIR, G_C, J_C))
    A = -jnp.exp(a_log.astype(f32)).reshape(N_DIR, G_C, J_C)
    h0 = h_st.astype(f32).reshape(bsz, N_DIR, G_C, J_C, P_C, N_C)
    y_f, hf = _ssd_scan(xs, bm, cm, dt[:, :, 0], dt[:, :, 0] * A[0], h0[:, 0])
    dt_b = _flip(dt[:, :, 1])
    y_b, hb = _ssd_scan(_flip(xs), _flip(bm), _flip(cm), dt_b, dt_b * A[1], h0[:, 1])
    y = y_f + _flip(y_b) + d_skip.astype(f32).reshape(G_C, J_C, 1) * xs
    y = y.reshape(bsz, t, D_INNER) * jax.nn.silu(z.astype(f32))
    y = y * lax.rsqrt(jnp.square(y).mean(-1, keepdims=True) + 1e-5) * norm_w
    fin = jnp.stack([hf.reshape(bsz, H_C, P_C, N_C), hb.reshape(bsz, H_C, P_C, N_C)], axis=1)
    return _mm3(y.astype(u.dtype), w_out), fin


def _expert_slab_copy(tab_ref, buf_ref, sem_ref, row, pair, slot):
    src = tab_ref.at[pl.ds(pl.multiple_of(row, SLAB_ROWS), SLAB_ROWS), :]
    return pltpu.make_async_copy(src, buf_ref.at[slot, pl.ds(pair * SLAB_PITCH, SLAB_ROWS), :], sem_ref.at[slot])


def _peer_gather_kernel(idx_ref, x_ref, g_ref, tab_ref, o_ref, buf_ref, sem_ref):
    n_tok = x_ref.shape[0]
    n_chunk = x_ref.shape[1] // LANE

    per_phase = PEER_PAIRS // (2 * n_chunk)

    def issue(t, slot, p0, p1):
        for p in range(p0, p1):
            _expert_slab_copy(tab_ref, buf_ref, sem_ref, idx_ref[t, p], p, slot).start(priority=p % 2)

    def wait(slot):
        n_rows = PEER_PAIRS * SLAB_ROWS
        pltpu.make_async_copy(tab_ref.at[pl.ds(0, n_rows), :], buf_ref.at[slot, pl.ds(0, n_rows), :], sem_ref.at[slot]).wait()

    def chunk(slot, c):
        return buf_ref[slot, pl.ds(c, PEER_PAIRS, stride=SLAB_PITCH), :]

    eye = lax.broadcasted_iota(jnp.int32, (PEER_PAIRS, PEER_PAIRS), 0) == lax.broadcasted_iota(jnp.int32, (PEER_PAIRS, PEER_PAIRS), 1)

    def compute(t, slot, t_next, slot_next):
        phase = [0]

        def issue_some():
            if t_next is not None:
                issue(t_next, slot_next, phase[0] * per_phase, (phase[0] + 1) * per_phase)
            phase[0] += 1

        wait(slot)
        gt = g_ref[pl.ds(t, 1), :]
        xt = x_ref[pl.ds(t, 1), :]
        acc = chunk(slot, 0) * xt[:, :LANE]
        issue_some()
        for c in range(1, n_chunk):
            acc = acc + chunk(slot, c) * xt[:, c * LANE:(c + 1) * LANE]
            issue_some()
        act = jnp.sum(acc, axis=1, keepdims=True)
        g_col = jnp.sum(jnp.where(eye, gt, 0.0), axis=1, keepdims=True)
        w = g_col * (0.5 * act * (1.0 + lax.erf(act * np.float32(np.sqrt(0.5)))))
        outs = []
        for c in range(n_chunk):
            outs.append(jnp.sum(w * chunk(slot, n_chunk + c), axis=0, keepdims=True))
            issue_some()
        o_ref[pl.ds(t, 1), :] = jnp.concatenate(outs, axis=1)

    ahead = PEER_NBUF - 1
    for s in range(ahead):
        issue(s, s, 0, PEER_PAIRS)

    def group(base, n_issue):
        for j in range(PEER_NBUF):
            if j < n_issue:
                compute(base + j, j, base + j + ahead, (j + ahead) % PEER_NBUF)
            else:
                compute(base + j, j, None, None)

    n_group = n_tok // PEER_NBUF

    def steady(gi, carry):
        group(gi * PEER_NBUF, PEER_NBUF)
        return carry

    lax.fori_loop(0, n_group - 1, steady, 0)
    group((n_group - 1) * PEER_NBUF, PEER_NBUF - ahead)


def _peer_gather(x, idx, g, uv_tab):
    n, d = x.shape
    return pl.pallas_call(
        _peer_gather_kernel,
        grid=(n // PEER_TB,),
        in_specs=[
            pl.BlockSpec((PEER_TB, PEER_PAIRS), lambda i: (i, 0), memory_space=pltpu.SMEM),
            pl.BlockSpec((PEER_TB, d), lambda i: (i, 0)),
            pl.BlockSpec((PEER_TB, PEER_PAIRS), lambda i: (i, 0)),
            pl.BlockSpec(memory_space=pl.ANY),
        ],
        out_specs=pl.BlockSpec((PEER_TB, d), lambda i: (i, 0)),
        out_shape=jax.ShapeDtypeStruct((n, d), jnp.float32),
        scratch_shapes=[pltpu.VMEM((PEER_NBUF, PEER_PAIRS * SLAB_PITCH, LANE), jnp.float32), pltpu.SemaphoreType.DMA((PEER_NBUF,))],
        name="peer_gather",
    )(idx * SLAB_ROWS, x, g, uv_tab)


def _top_rows(s, k, payload=None):
    n_rows = s.shape[0]
    row = lax.broadcasted_iota(jnp.int32, s.shape, 0)
    out_row = lax.broadcasted_iota(jnp.int32, (k, s.shape[1]), 0)
    vals = jnp.zeros((k, s.shape[1]), jnp.float32)
    sel = jnp.zeros((k, s.shape[1]), jnp.int32)
    for j in range(k):
        m = jnp.max(s, axis=0, keepdims=True)
        r = jnp.min(jnp.where(s == m, row, n_rows), axis=0, keepdims=True)
        hit = row == r
        s = jnp.where(hit, -jnp.inf, s)
        what = r if payload is None else jnp.max(jnp.where(hit, payload, -1), axis=0, keepdims=True)
        vals = jnp.where(out_row == j, m, vals)
        sel = jnp.where(out_row == j, what, sel)
    return vals, sel


def _peer_select_kernel(x_ref, sc_ref, sh_ref, wq_ref, keys_ref, xm_ref, idx_ref, g_ref, q_scr, sv_scr, si_scr, bi_scr, bg_scr):
    f32 = jnp.float32
    n_hc = 2 * H_P
    h = x_ref[...] * (1.0 + sc_ref[0]) + sh_ref[0]
    xm_ref[...] = h
    q = jnp.dot(h.astype(jnp.bfloat16), wq_ref[...], preferred_element_type=f32)
    for hc in range(n_hc):
        q_scr[hc] = q[:, hc * D_HALF:(hc + 1) * D_HALF].astype(jnp.bfloat16)

    def sub_key_top(hc, carry):
        s = lax.dot_general(keys_ref[hc], q_scr[hc], (((1,), (1,)), ((), ())), preferred_element_type=f32)
        sv_scr[hc], si_scr[hc] = _top_rows(s, TOPK)
        return carry

    lax.fori_loop(0, n_hc, sub_key_top, 0)

    def pair_top(hd, carry):
        sv0, sv1 = sv_scr[2 * hd], sv_scr[2 * hd + 1]
        si0, si1 = si_scr[2 * hd], si_scr[2 * hd + 1]
        half = TOPK // 2
        cand = [sv0[0:1] + sv1]
        cidx = [si0[0:1] * N_KEYS + si1]
        for a in range(1, half):
            cand.append(sv0[a:a + 1] + sv1[:half])
            cidx.append(si0[a:a + 1] * N_KEYS + si1[:half])
        cand.append(sv0[half:] + sv1[0:1])
        cidx.append(si0[half:] * N_KEYS + si1[0:1])
        best, eid = _top_rows(jnp.concatenate(cand, axis=0), TOPK, payload=jnp.concatenate(cidx, axis=0))
        e = jnp.exp(best - best[0:1])
        bi_scr[hd] = eid
        bg_scr[hd] = e / jnp.sum(e, axis=0, keepdims=True)
        return carry

    lax.fori_loop(0, H_P, pair_top, 0)
    tm = x_ref.shape[0]
    idx_ref[...] = bi_scr[...].reshape(PEER_PAIRS, tm).T
    g_ref[...] = bg_scr[...].reshape(PEER_PAIRS, tm).T


def _peer_select(x, sc, sh, w_q, sub_keys):
    n_seq, seq_len, d = x.shape
    n = n_seq * seq_len
    n_hc = 2 * H_P
    per_seq = seq_len // SEL_TM
    tok = lambda i: (i, 0)
    seq = lambda i: (i // per_seq, 0, 0)
    return pl.pallas_call(
        _peer_select_kernel,
        grid=(n // SEL_TM,),
        in_specs=[
            pl.BlockSpec((SEL_TM, d), tok),
            pl.BlockSpec((1, 1, d), seq),
            pl.BlockSpec((1, 1, d), seq),
            pl.BlockSpec((d, H_P * D_QK), lambda i: (0, 0)),
            pl.BlockSpec((n_hc, N_KEYS, D_HALF), lambda i: (0, 0, 0)),
        ],
        out_specs=[pl.BlockSpec((SEL_TM, d), tok), pl.BlockSpec((SEL_TM, PEER_PAIRS), tok), pl.BlockSpec((SEL_TM, PEER_PAIRS), tok)],
        out_shape=[jax.ShapeDtypeStruct((n, d), jnp.float32), jax.ShapeDtypeStruct((n, PEER_PAIRS), jnp.int32), jax.ShapeDtypeStruct((n, PEER_PAIRS), jnp.float32)],
        scratch_shapes=[
            pltpu.VMEM((n_hc, SEL_TM, D_HALF), jnp.bfloat16),
            pltpu.VMEM((n_hc, TOPK, SEL_TM), jnp.float32),
            pltpu.VMEM((n_hc, TOPK, SEL_TM), jnp.int32),
            pltpu.VMEM((H_P, TOPK, SEL_TM), jnp.int32),
            pltpu.VMEM((H_P, TOPK, SEL_TM), jnp.float32),
        ],
        name="peer_select",
    )(x.reshape(n, d), sc, sh, w_q.astype(jnp.bfloat16), sub_keys.reshape(n_hc, N_KEYS, D_HALF).astype(jnp.bfloat16))


def _peer(x, sc, sh, w_q, sub_keys, uv_tab):
    xm, idx, g = _peer_select(x, sc, sh, w_q, sub_keys)
    return _peer_gather(xm, idx, g, uv_tab).reshape(x.shape)


def kernel(x_prompt, x_sample, c, c_ctx, state_mlstm_C, state_mlstm_n, state_mlstm_m, state_ret, state_ssd, w_mod, b_mod, ln1_g, ln1_b, ln2_g, ln2_b, even_w_in, mlstm_i_bias, mlstm_f_bias, ret_log_decay, even_gn_a, even_gn_b, even_w_out, odd_w_in, odd_conv_w, odd_conv_b, odd_dt_bias, odd_a_log, odd_d, odd_norm_w, odd_w_out, peer_w_q, peer_sub_keys, peer_u, peer_v):
    f32 = jnp.float32
    bp = x_prompt.shape[0]
    xp = x_prompt
    xs = x_sample + _grid_pos_embed(x_sample.shape[1], x_sample.dtype)[None]
    new_C, new_n, new_m, new_S, new_h = [], [], [], [], []
    for l in range(DEPTH):
        sh1p, sc1p, g1p, sh2p, sc2p, g2p = _modulation(c_ctx[None, :], w_mod[l], b_mod[l])
        sh1s, sc1s, g1s, sh2s, sc2s, g2s = _modulation(c, w_mod[l], b_mod[l])
        hp = xp * (1 + sc1p) + sh1p
        hs = xs * (1 + sc1s) + sh1s
        j = l // 2
        if l % 2 == 0:
            prm = (even_w_in[j], mlstm_i_bias[j], mlstm_f_bias[j], ret_log_decay[j], even_gn_a[j], even_gn_b[j], even_w_out[j])
            mp, fin = _even_mixer(hp, *prm, jnp.zeros((bp, N_DIR, H_A, DK_A, DV_A), f32), jnp.zeros((bp, N_DIR, H_A, DK_A), f32), jnp.zeros((bp, N_DIR, H_A), f32), jnp.zeros((bp, N_DIR, H_B, DK_B, DV_B), f32))
            ms, _ = _even_mixer(hs, *prm, state_mlstm_C[:, j], state_mlstm_n[:, j], state_mlstm_m[:, j], state_ret[:, j])
            new_C.append(fin[0])
            new_n.append(fin[1])
            new_m.append(fin[2])
            new_S.append(fin[3])
        else:
            prm = (odd_w_in[j], odd_conv_w[j], odd_conv_b[j], odd_dt_bias[j], odd_a_log[j], odd_d[j], odd_norm_w[j], odd_w_out[j])
            mp, fin = _odd_mixer(hp, *prm, jnp.zeros((bp, N_DIR, H_C, P_C, N_C), f32))
            ms, _ = _odd_mixer(hs, *prm, state_ssd[:, j])
            new_h.append(fin)
        xp = _layer_norm(ALPHA * xp + g1p * mp, ln1_g[l], ln1_b[l])
        xs = _layer_norm(ALPHA * xs + g1s * ms, ln1_g[l], ln1_b[l])
        uv_tab = jnp.concatenate([peer_u[l], peer_v[l]], axis=1).reshape(-1, LANE)
        bcast = lambda m: jnp.broadcast_to(m, (bp, 1, D_MODEL))
        fp = _peer(xp, bcast(sc2p), bcast(sh2p), peer_w_q[l], peer_sub_keys[l], uv_tab)
        fs = _peer(xs, sc2s, sh2s, peer_w_q[l], peer_sub_keys[l], uv_tab)
        xp = _layer_norm(ALPHA * xp + g2p * fp, ln2_g[l], ln2_b[l])
        xs = _layer_norm(ALPHA * xs + g2s * fs, ln2_g[l], ln2_b[l])
    return (xp, xs, jnp.stack(new_C, axis=1), jnp.stack(new_n, axis=1), jnp.stack(new_m, axis=1), jnp.stack(new_S, axis=1), jnp.stack(new_h, axis=1))
```

```python
import functools

import jax
import jax.numpy as jnp
import numpy as np
from jax import lax
from jax.experimental import pallas as pl
from jax.experimental.pallas import tpu as pltpu

D_MODEL = 1024
DEPTH = 4
GRID_W = 64
CHUNK = 64
N_DIR = 2
ALPHA = (2.0 * DEPTH) ** 0.25
H_A = 4
DK_A = 128
DV_A = 128
H_B = 4
DK_B = 128
DV_B = 128
W_A = H_A * DV_A
W_B = H_B * DV_B
EVEN_SIZES = (H_A * DK_A, H_A * DK_A, W_A, W_A, N_DIR * H_A, N_DIR * H_A, H_B * DK_B, H_B * DK_B, W_B, W_B)
EVEN_SPLITS = [int(s) for s in np.cumsum(EVEN_SIZES)[:-1]]
D_INNER = 2 * D_MODEL
P_C = 64
H_C = D_INNER // P_C
G_C = 4
J_C = H_C // G_C
N_C = 128
D_CONV = 5
CONV_DIM = D_INNER + 2 * G_C * N_C
N_KEYS = 128
H_P = 8
D_QK = 256
D_HALF = D_QK // 2
TOPK = 16

LANE = 128
ROW_TILE = 256
PEER_PAIRS = H_P * TOPK
SEL_TM = 256
PEER_TB = 128
PEER_NBUF = 8
SLAB_ROWS = 2 * D_MODEL // LANE
SLAB_PITCH = SLAB_ROWS + 4


def _mod_proj_kernel(x_ref, sc_ref, sh_ref, w_ref, o_ref):
    h = x_ref[...] * (1.0 + sc_ref[0]) + sh_ref[0]
    o_ref[...] = jnp.dot(h.astype(jnp.bfloat16), w_ref[...], preferred_element_type=jnp.float32)


def _mod_proj(x, sc, sh, wb):
    n_seq, seq_len, d = x.shape
    n = n_seq * seq_len
    n_out = wb.shape[1]
    per_seq = seq_len // ROW_TILE
    seq = lambda i: (i // per_seq, 0, 0)
    return pl.pallas_call(
        _mod_proj_kernel,
        grid=(n // ROW_TILE,),
        in_specs=[pl.BlockSpec((ROW_TILE, d), lambda i: (i, 0)), pl.BlockSpec((1, 1, d), seq), pl.BlockSpec((1, 1, d), seq), pl.BlockSpec((d, n_out), lambda i: (0, 0))],
        out_specs=pl.BlockSpec((ROW_TILE, n_out), lambda i: (i, 0)),
        out_shape=jax.ShapeDtypeStruct((n, n_out), jnp.float32),
        name="mod_proj",
    )(x.reshape(n, d), sc, sh, wb)


def _layer_norm_rows(y, g, b, eps=1e-5):
    mu = jnp.mean(y, axis=-1, keepdims=True)
    yc = y - mu
    var = jnp.mean(yc * yc, axis=-1, keepdims=True)
    return yc * lax.rsqrt(var + eps) * g + b


EVEN_MAIN = 8 * W_A
EVEN_COLS = EVEN_MAIN + LANE
N_HD = N_DIR * H_A


def _even_scan_kernel(pf_ref, pb_ref, c0_ref, n0_ref, m0_ref, s0_ref, gb_ref, lg_ref,
                      hf_ref, hb_ref, cf_ref, nf_ref, mf_ref, sf_ref, c_scr, n_scr, m_scr, s_scr):
    f32, bf16 = jnp.float32, jnp.bfloat16
    ci = pl.program_id(1)

    @pl.when(ci == 0)
    def _():
        c_scr[...] = c0_ref[0]
        n_scr[...] = n0_ref[0]
        m_scr[...] = m0_ref[0]
        s_scr[...] = s0_ref[0]

    t_i = lax.broadcasted_iota(jnp.int32, (CHUNK, CHUNK), 0)
    s_i = lax.broadcasted_iota(jnp.int32, (CHUNK, CHUNK), 1)
    t_col = lax.broadcasted_iota(jnp.int32, (CHUNK, 1), 0).astype(f32)
    nt = (((1,), (1,)), ((), ()))
    tn = (((0,), (0,)), ((), ()))
    hi = lax.Precision.HIGHEST

    for d, (p_ref, h_ref) in enumerate(((pf_ref, hf_ref), (pb_ref, hb_ref))):
        mask = (s_i <= t_i) if d == 0 else (s_i >= t_i)
        tri = mask.astype(f32)
        last = CHUNK - 1 if d == 0 else 0
        gates = p_ref[:, EVEN_MAIN:] + gb_ref[...]
        lf = jax.nn.log_sigmoid(gates)
        b_cols = jnp.dot(tri, lf, precision=hi, preferred_element_type=f32)
        b_rows = lax.dot_general(lf.T, tri, nt, precision=hi, preferred_element_type=f32)
        li_rows = gates.T
        for hd in range(H_A):
            j = d * H_A + hd
            q = p_ref[:, hd * DK_A:(hd + 1) * DK_A]
            k = p_ref[:, W_A + hd * DK_A:W_A + (hd + 1) * DK_A] * DK_A ** -0.5
            v = p_ref[:, 2 * W_A + hd * DV_A:2 * W_A + (hd + 1) * DV_A]
            qb_, kb_, vb_ = q.astype(bf16), k.astype(bf16), v.astype(bf16)
            b_col = b_cols[:, N_HD + j:N_HD + j + 1]
            b_row = b_rows[N_HD + j:N_HD + j + 1, :]
            li_col = gates[:, j:j + 1]
            li_row = li_rows[j:j + 1, :]
            m_prev = m_scr[j][:, 0:1]
            dlog = jnp.where(mask, b_col - b_row + li_row, -jnp.inf)
            inter = b_col + m_prev
            m_t = jnp.maximum(inter, jnp.max(dlog, axis=1, keepdims=True))
            s = lax.dot_general(qb_, kb_, nt, preferred_element_type=f32) * jnp.exp(dlog - m_t)
            w_int = jnp.exp(inter - m_t)
            num = jnp.dot(s.astype(bf16), vb_, preferred_element_type=f32) + w_int * jnp.dot(qb_, c_scr[j].astype(bf16), preferred_element_type=f32)
            den = jnp.sum(s, axis=1, keepdims=True) + w_int * jnp.sum(q * n_scr[j], axis=1, keepdims=True)
            h_ref[:, hd * DV_A:(hd + 1) * DV_A] = num / jnp.maximum(jnp.abs(den), jnp.exp(-m_t))
            b_last = b_col[last:last + 1]
            g_col = b_last - b_col + li_col
            m_new = jnp.maximum(b_last + m_prev, jnp.max(g_col, axis=0, keepdims=True))
            wg = jnp.exp(g_col - m_new)
            dec = jnp.exp(b_last + m_prev - m_new)
            c_scr[j] = dec * c_scr[j] + lax.dot_general((k * wg).astype(bf16), vb_, tn, preferred_element_type=f32)
            n_scr[j] = dec * n_scr[j] + jnp.sum(wg * k, axis=0, keepdims=True)
            m_scr[j] = jnp.broadcast_to(m_new, (1, DK_A))
        diff = (t_i - s_i if d == 0 else s_i - t_i).astype(f32)
        tau = t_col if d == 0 else (CHUNK - 1.0) - t_col
        for hd in range(H_B):
            j = d * H_B + hd
            lg = lg_ref[d, hd]
            base = 4 * W_A
            q = p_ref[:, base + hd * DK_B:base + (hd + 1) * DK_B] * DK_B ** -0.5
            k = p_ref[:, base + W_B + hd * DK_B:base + W_B + (hd + 1) * DK_B]
            v = p_ref[:, base + 2 * W_B + hd * DV_B:base + 2 * W_B + (hd + 1) * DV_B]
            qb_, vb_ = q.astype(bf16), v.astype(bf16)
            decay = jnp.where(diff >= 0, jnp.exp(jnp.maximum(diff, 0.0) * lg), 0.0)
            xi = jnp.exp((tau + 1.0) * lg)
            zeta = jnp.exp((CHUNK - 1.0 - tau) * lg)
            g_end = jnp.exp(jnp.full((1, 1), CHUNK, f32) * lg)
            s = lax.dot_general(qb_, k.astype(bf16), nt, preferred_element_type=f32) * decay
            o = jnp.dot(s.astype(bf16), vb_, preferred_element_type=f32) + xi * jnp.dot(qb_, s_scr[j].astype(bf16), preferred_element_type=f32)
            h_ref[:, W_A + hd * DV_B:W_A + (hd + 1) * DV_B] = o
            s_scr[j] = g_end * s_scr[j] + lax.dot_general((k * zeta).astype(bf16), vb_, tn, preferred_element_type=f32)

    @pl.when(ci == pl.num_programs(1) - 1)
    def _():
        cf_ref[0] = c_scr[...]
        nf_ref[0] = n_scr[...]
        mf_ref[0] = m_scr[...]
        sf_ref[0] = s_scr[...]


def _even_scan(proj, n_seq, seq_len, c0, n0, m0, s0, gate_bias, log_gamma):
    nc = seq_len // CHUNK
    f32 = jnp.float32
    st4 = lambda s, c: (s, 0, 0, 0)
    big = pl.BlockSpec((1, N_HD, DK_A, DV_A), st4)
    small = pl.BlockSpec((1, N_HD, 1, DK_A), st4)
    fwd = lambda s, c: (s * nc + c, 0)
    bwd = lambda s, c: (s * nc + (nc - 1 - c), 0)
    n = n_seq * seq_len
    return pl.pallas_call(
        _even_scan_kernel,
        grid=(n_seq, nc),
        in_specs=[pl.BlockSpec((CHUNK, EVEN_COLS), fwd), pl.BlockSpec((CHUNK, EVEN_COLS), bwd), big, small, small, big,
                  pl.BlockSpec((1, LANE), lambda s, c: (0, 0)), pl.BlockSpec(memory_space=pltpu.SMEM)],
        out_specs=[pl.BlockSpec((CHUNK, W_A + W_B), fwd), pl.BlockSpec((CHUNK, W_A + W_B), bwd), big, small, small, big],
        out_shape=[jax.ShapeDtypeStruct((n, W_A + W_B), f32), jax.ShapeDtypeStruct((n, W_A + W_B), f32),
                   jax.ShapeDtypeStruct(c0.shape, f32), jax.ShapeDtypeStruct(n0.shape, f32), jax.ShapeDtypeStruct(m0.shape, f32), jax.ShapeDtypeStruct(s0.shape, f32)],
        scratch_shapes=[pltpu.VMEM((N_HD, DK_A, DV_A), f32), pltpu.VMEM((N_HD, 1, DK_A), f32), pltpu.VMEM((N_HD, 1, DK_A), f32), pltpu.VMEM((N_HD, DK_B, DV_B), f32)],
        name="even_scan",
    )(proj, proj, c0, n0, m0, s0, gate_bias, log_gamma)


def _even_post_kernel(hf_ref, hb_ref, oa_ref, gb_ref, x_ref, g1_ref, gn_ref, w_ref, lng_ref, lnb_ref, o_ref):
    hs = hf_ref[...] + hb_ref[...]
    parts = []
    for hd in range(H_A + H_B):
        hh = hs[:, hd * DV_A:(hd + 1) * DV_A]
        mu = jnp.mean(hh, axis=-1, keepdims=True)
        hc = hh - mu
        var = jnp.mean(hc * hc, axis=-1, keepdims=True)
        parts.append(hc * lax.rsqrt(var + 1e-5))
    normed = jnp.concatenate(parts, axis=1) * gn_ref[...]
    gb = gb_ref[...]
    act = jnp.concatenate([jax.nn.sigmoid(oa_ref[...]), gb * jax.nn.sigmoid(gb)], axis=1)
    ms = jnp.dot((normed * act).astype(jnp.bfloat16), w_ref[...], preferred_element_type=jnp.float32)
    o_ref[...] = _layer_norm_rows(ALPHA * x_ref[...] + g1_ref[0] * ms, lng_ref[...], lnb_ref[...])


def _even_post(hf, hb, proj, x, g1, gn, w_out_b, ln_g, ln_b):
    n_seq, seq_len, d = x.shape
    n = n_seq * seq_len
    per_seq = seq_len // ROW_TILE
    row = lambda i: (i, 0)
    full = lambda i: (0, 0)
    return pl.pallas_call(
        _even_post_kernel,
        grid=(n // ROW_TILE,),
        in_specs=[pl.BlockSpec((ROW_TILE, W_A + W_B), row), pl.BlockSpec((ROW_TILE, W_A + W_B), row),
                  pl.BlockSpec((ROW_TILE, W_A), lambda i: (i, 3)), pl.BlockSpec((ROW_TILE, W_B), lambda i: (i, 7)),
                  pl.BlockSpec((ROW_TILE, d), row), pl.BlockSpec((1, 1, d), lambda i: (i // per_seq, 0, 0)),
                  pl.BlockSpec((1, W_A + W_B), full), pl.BlockSpec((W_A + W_B, d), full), pl.BlockSpec((1, d), full), pl.BlockSpec((1, d), full)],
        out_specs=pl.BlockSpec((ROW_TILE, d), row),
        out_shape=jax.ShapeDtypeStruct((n, d), jnp.float32),
        name="even_post",
    )(hf, hb, proj, proj, x.reshape(n, d), g1, gn, w_out_b, ln_g, ln_b).reshape(x.shape)


def _even_weights(w_in, i_bias, f_bias, log_decay, gn_a, gn_b, w_out):
    qa, ka, va, oa, ia, fa, qb, kb, vb, gb = jnp.split(w_in, EVEN_SPLITS, axis=1)
    gates = jnp.pad(jnp.concatenate([ia, fa], axis=1), ((0, 0), (0, LANE - 2 * N_HD)))
    w_in_b = jnp.concatenate([qa, ka, va, oa, qb, kb, vb, gb, gates], axis=1).astype(jnp.bfloat16)
    gate_bias = jnp.pad(jnp.concatenate([i_bias.reshape(-1), f_bias.reshape(-1)]), (0, LANE - 2 * N_HD)).reshape(1, LANE)
    log_gamma = -jnp.exp(log_decay.astype(jnp.float32))
    gn = jnp.concatenate([gn_a.reshape(-1), gn_b.reshape(-1)]).reshape(1, W_A + W_B)
    return w_in_b, gate_bias, log_gamma, gn, w_out.astype(jnp.bfloat16)


def _even_layer(x, sc1, sh1, g1, wts, ln_g, ln_b, c0, n0, m0, s0):
    w_in_b, gate_bias, log_gamma, gn, w_out_b = wts
    n_seq, seq_len, d = x.shape
    proj = _mod_proj(x, sc1, sh1, w_in_b)
    st = lambda a: a.reshape(n_seq, N_HD, DK_A, DV_A)
    vec = lambda a: a.reshape(n_seq, N_HD, 1, DK_A)
    m0b = jnp.broadcast_to(m0.reshape(n_seq, N_HD, 1, 1), (n_seq, N_HD, 1, DK_A))
    hf, hb, cf, nf, mf, sf = _even_scan(proj, n_seq, seq_len, st(c0), vec(n0), m0b, st(s0), gate_bias, log_gamma)
    x_new = _even_post(hf, hb, proj, x, g1, gn, w_out_b, ln_g.reshape(1, d), ln_b.reshape(1, d))
    fin = (cf.reshape(n_seq, N_DIR, H_A, DK_A, DV_A), nf.reshape(n_seq, N_DIR, H_A, DK_A), mf[:, :, 0, 0].reshape(n_seq, N_DIR, H_A), sf.reshape(n_seq, N_DIR, H_B, DK_B, DV_B))
    return x_new, fin


ODD_DT = D_INNER + CONV_DIM
ODD_COLS = ODD_DT + LANE
CONV_TILE = 1024
HALO = 8
GW = J_C * P_C


def _conv_kernel(x_ref, prev_ref, next_ref, w_ref, b_ref, o_ref, *, per_seq):
    i = pl.program_id(0)
    pad = D_CONV // 2
    rows = x_ref.shape[0]
    first = lax.rem(i, per_seq) == 0
    last = lax.rem(i, per_seq) == per_seq - 1
    prev = jnp.where(first, 0.0, prev_ref[HALO - pad:, :])
    nxt = jnp.where(last, 0.0, next_ref[:pad, :])
    ext = jnp.concatenate([prev, x_ref[...], nxt], axis=0)
    acc = b_ref[...] + ext[0:rows] * w_ref[0:1, :]
    for k in range(1, D_CONV):
        acc = acc + ext[k:k + rows] * w_ref[k:k + 1, :]
    o_ref[...] = acc * jax.nn.sigmoid(acc)


def _conv_silu(proj, seq_len, conv_w, conv_b):
    n = proj.shape[0]
    per_seq = seq_len // ROW_TILE
    c0 = D_INNER // CONV_TILE
    hb = ROW_TILE // HALO
    n_hb = n // HALO
    return pl.pallas_call(
        functools.partial(_conv_kernel, per_seq=per_seq),
        grid=(n // ROW_TILE, CONV_DIM // CONV_TILE),
        in_specs=[pl.BlockSpec((ROW_TILE, CONV_TILE), lambda i, j: (i, c0 + j)),
                  pl.BlockSpec((HALO, CONV_TILE), lambda i, j: (jnp.maximum(i * hb - 1, 0), c0 + j)),
                  pl.BlockSpec((HALO, CONV_TILE), lambda i, j: (jnp.minimum((i + 1) * hb, n_hb - 1), c0 + j)),
                  pl.BlockSpec((D_CONV, CONV_TILE), lambda i, j: (0, j)), pl.BlockSpec((1, CONV_TILE), lambda i, j: (0, j))],
        out_specs=pl.BlockSpec((ROW_TILE, CONV_TILE), lambda i, j: (i, j)),
        out_shape=jax.ShapeDtypeStruct((n, CONV_DIM), jnp.float32),
        name="conv_silu",
    )(proj, proj, proj, conv_w, conv_b.reshape(1, CONV_DIM))


def _spread_heads(mat, col0):
    lane = lax.broadcasted_iota(jnp.int32, (mat.shape[0], LANE), 1)
    per_tile = LANE // P_C
    tiles = []
    for j in range(0, J_C, per_tile):
        t = mat[:, col0 + j:col0 + j + 1]
        for r in range(1, per_tile):
            t = jnp.where(lane < r * P_C, t, mat[:, col0 + j + r:col0 + j + r + 1])
        tiles.append(jnp.broadcast_to(t, (mat.shape[0], LANE)))
    return jnp.concatenate(tiles, axis=1)


def _ssd_scan_kernel(af_ref, ab_ref, dtf_ref, dtb_ref, bias_ref, alog_ref, h0_ref, yf_ref, yb_ref, hf_ref, h_scr):
    f32, bf16 = jnp.float32, jnp.bfloat16
    ci = pl.program_id(1)

    @pl.when(ci == 0)
    def _():
        h_scr[...] = h0_ref[0]

    t_i = lax.broadcasted_iota(jnp.int32, (CHUNK, CHUNK), 0)
    s_i = lax.broadcasted_iota(jnp.int32, (CHUNK, CHUNK), 1)
    nt = (((1,), (1,)), ((), ()))
    tn = (((0,), (0,)), ((), ()))
    hi = lax.Precision.HIGHEST
    a_neg = -jnp.exp(alog_ref[...])

    for d, (x_ref, dt_ref, y_ref) in enumerate(((af_ref, dtf_ref, yf_ref), (ab_ref, dtb_ref, yb_ref))):
        mask = (s_i <= t_i) if d == 0 else (s_i >= t_i)
        tri = mask.astype(f32)
        last = CHUNK - 1 if d == 0 else 0
        dtv = jax.nn.softplus(dt_ref[...] + bias_ref[...])
        a = dtv * a_neg
        ac_cols = jnp.dot(tri, a, precision=hi, preferred_element_type=f32)
        ac_rows = lax.dot_general(a.T, tri, nt, precision=hi, preferred_element_type=f32)
        dt_rows = dtv.T
        e_cols = jnp.exp(ac_cols)
        w_cols = jnp.exp(ac_cols[last:last + 1] - ac_cols) * dtv
        for g in range(G_C):
            col0 = d * H_C + g * J_C
            bg = x_ref[:, D_INNER + g * N_C:D_INNER + (g + 1) * N_C].astype(bf16)
            cg = x_ref[:, D_INNER + G_C * N_C + g * N_C:D_INNER + G_C * N_C + (g + 1) * N_C].astype(bf16)
            xg = x_ref[:, g * GW:(g + 1) * GW]
            cb = lax.dot_general(cg, bg, nt, preferred_element_type=f32)
            e_g = _spread_heads(e_cols, col0)
            y_ref[:, g * GW:(g + 1) * GW] = e_g * jnp.dot(cg, h_scr[d, g].astype(bf16), preferred_element_type=f32)
            for j in range(J_C):
                col = col0 + j
                seg = jnp.where(mask, ac_cols[:, col:col + 1] - ac_rows[col:col + 1, :], -jnp.inf)
                s = cb * jnp.exp(seg) * dt_rows[col:col + 1, :]
                lo = g * GW + j * P_C
                y_ref[:, lo:lo + P_C] += jnp.dot(s.astype(bf16), xg[:, j * P_C:(j + 1) * P_C].astype(bf16), preferred_element_type=f32)
            xw = (xg * _spread_heads(w_cols, col0)).astype(bf16)
            h_scr[d, g] = e_g[last:last + 1] * h_scr[d, g] + lax.dot_general(bg, xw, tn, preferred_element_type=f32)

    @pl.when(ci == pl.num_programs(1) - 1)
    def _():
        hf_ref[0] = h_scr[...]


def _ssd_chunk_scan(xa, proj, n_seq, seq_len, dt_bias_row, a_log_row, h0):
    nc = seq_len // CHUNK
    n = n_seq * seq_len
    f32 = jnp.float32
    fwd = lambda s, c: (s * nc + c, 0)
    bwd = lambda s, c: (s * nc + (nc - 1 - c), 0)
    dcol = ODD_DT // LANE
    st = pl.BlockSpec((1, N_DIR, G_C, N_C, GW), lambda s, c: (s, 0, 0, 0, 0))
    row = pl.BlockSpec((1, LANE), lambda s, c: (0, 0))
    return pl.pallas_call(
        _ssd_scan_kernel,
        grid=(n_seq, nc),
        in_specs=[pl.BlockSpec((CHUNK, CONV_DIM), fwd), pl.BlockSpec((CHUNK, CONV_DIM), bwd),
                  pl.BlockSpec((CHUNK, LANE), lambda s, c: (s * nc + c, dcol)), pl.BlockSpec((CHUNK, LANE), lambda s, c: (s * nc + (nc - 1 - c), dcol)),
                  row, row, st],
        out_specs=[pl.BlockSpec((CHUNK, D_INNER), fwd), pl.BlockSpec((CHUNK, D_INNER), bwd), st],
        out_shape=[jax.ShapeDtypeStruct((n, D_INNER), f32), jax.ShapeDtypeStruct((n, D_INNER), f32), jax.ShapeDtypeStruct(h0.shape, f32)],
        scratch_shapes=[pltpu.VMEM((N_DIR, G_C, N_C, GW), f32)],
        name="ssd_scan",
    )(xa, xa, proj, proj, dt_bias_row, a_log_row, h0)


def _odd_post_kernel(yf_ref, yb_ref, xs_ref, z_ref, x_ref, g1_ref, dsk_ref, nw_ref, w_ref, lng_ref, lnb_ref, o_ref):
    z = z_ref[...]
    y = (yf_ref[...] + yb_ref[...] + dsk_ref[...] * xs_ref[...]) * (z * jax.nn.sigmoid(z))
    y = y * lax.rsqrt(jnp.mean(y * y, axis=-1, keepdims=True) + 1e-5) * nw_ref[...]
    ms = jnp.dot(y.astype(jnp.bfloat16), w_ref[...], preferred_element_type=jnp.float32)
    o_ref[...] = _layer_norm_rows(ALPHA * x_ref[...] + g1_ref[0] * ms, lng_ref[...], lnb_ref[...])


def _odd_post(yf, yb, xa, proj, x, g1, d_row, norm_w, w_out_b, ln_g, ln_b):
    n_seq, seq_len, d = x.shape
    n = n_seq * seq_len
    per_seq = seq_len // ROW_TILE
    row = lambda i: (i, 0)
    full = lambda i: (0, 0)
    wide = pl.BlockSpec((ROW_TILE, D_INNER), row)
    return pl.pallas_call(
        _odd_post_kernel,
        grid=(n // ROW_TILE,),
        in_specs=[wide, wide, wide, wide, pl.BlockSpec((ROW_TILE, d), row), pl.BlockSpec((1, 1, d), lambda i: (i // per_seq, 0, 0)),
                  pl.BlockSpec((1, D_INNER), full), pl.BlockSpec((1, D_INNER), full), pl.BlockSpec((D_INNER, d), full), pl.BlockSpec((1, d), full), pl.BlockSpec((1, d), full)],
        out_specs=pl.BlockSpec((ROW_TILE, d), row),
        out_shape=jax.ShapeDtypeStruct((n, d), jnp.float32),
        name="odd_post",
    )(yf, yb, xa, proj, x.reshape(n, d), g1, d_row, norm_w, w_out_b, ln_g, ln_b).reshape(x.shape)


def _odd_weights(w_in, dt_bias, a_log, d_skip, norm_w, w_out):
    w_in_b = jnp.pad(w_in, ((0, 0), (0, ODD_COLS - w_in.shape[1]))).astype(jnp.bfloat16)
    lane_row = lambda a: jnp.pad(a.astype(jnp.float32).reshape(-1), (0, LANE - N_DIR * H_C)).reshape(1, LANE)
    d_row = jnp.repeat(d_skip.astype(jnp.float32), P_C).reshape(1, D_INNER)
    return w_in_b, lane_row(dt_bias), lane_row(a_log), d_row, norm_w.reshape(1, D_INNER), w_out.astype(jnp.bfloat16)


def _odd_layer(x, sc1, sh1, g1, wts, conv_w, conv_b, ln_g, ln_b, h0):
    w_in_b, dt_row, alog_row, d_row, nw_row, w_out_b = wts
    n_seq, seq_len, d = x.shape
    proj = _mod_proj(x, sc1, sh1, w_in_b)
    xa = _conv_silu(proj, seq_len, conv_w, conv_b)
    to_scan = lambda h: h.reshape(n_seq, N_DIR, G_C, J_C, P_C, N_C).transpose(0, 1, 2, 5, 3, 4).reshape(n_seq, N_DIR, G_C, N_C, GW)
    yf, yb, hfin = _ssd_chunk_scan(xa, proj, n_seq, seq_len, dt_row, alog_row, to_scan(h0))
    x_new = _odd_post(yf, yb, xa, proj, x, g1, d_row, nw_row, w_out_b, ln_g.reshape(1, d), ln_b.reshape(1, d))
    fin = hfin.reshape(n_seq, N_DIR, G_C, N_C, J_C, P_C).transpose(0, 1, 2, 4, 5, 3).reshape(n_seq, N_DIR, H_C, P_C, N_C)
    return x_new, fin


def _modulation(cvec, w, b):
    m = jax.nn.silu(cvec) @ w + b
    return [t[:, None, :] for t in jnp.split(m, 6, axis=-1)]


def _grid_pos_embed(n_tok, dtype):
    rows = n_tok // GRID_W
    r = jnp.repeat(jnp.arange(rows, dtype=jnp.float32), GRID_W)
    col = jnp.tile(jnp.arange(GRID_W, dtype=jnp.float32), rows)
    quarter = D_MODEL // 4
    freqs = 1.0 / (10000.0 ** (jnp.arange(quarter, dtype=jnp.float32) / quarter))
    er = r[:, None] * freqs
    ec = col[:, None] * freqs
    return jnp.concatenate([jnp.sin(er), jnp.cos(er), jnp.sin(ec), jnp.cos(ec)], axis=-1).astype(dtype)


def _expert_slab_copy(tab_ref, buf_ref, sem_ref, row, pair, slot):
    src = tab_ref.at[pl.ds(pl.multiple_of(row, SLAB_ROWS), SLAB_ROWS), :]
    return pltpu.make_async_copy(src, buf_ref.at[slot, pl.ds(pair * SLAB_PITCH, SLAB_ROWS), :], sem_ref.at[slot])


def _peer_gather_kernel(idx_ref, x_ref, g_ref, tab_ref, o_ref, buf_ref, sem_ref):
    n_tok = x_ref.shape[0]
    n_chunk = x_ref.shape[1] // LANE

    per_phase = PEER_PAIRS // (2 * n_chunk)

    def issue(t, slot, p0, p1):
        for p in range(p0, p1):
            _expert_slab_copy(tab_ref, buf_ref, sem_ref, idx_ref[t, p], p, slot).start(priority=p % 2)

    def wait(slot):
        n_rows = PEER_PAIRS * SLAB_ROWS
        pltpu.make_async_copy(tab_ref.at[pl.ds(0, n_rows), :], buf_ref.at[slot, pl.ds(0, n_rows), :], sem_ref.at[slot]).wait()

    def chunk(slot, c):
        return buf_ref[slot, pl.ds(c, PEER_PAIRS, stride=SLAB_PITCH), :]

    eye = lax.broadcasted_iota(jnp.int32, (PEER_PAIRS, PEER_PAIRS), 0) == lax.broadcasted_iota(jnp.int32, (PEER_PAIRS, PEER_PAIRS), 1)

    def compute(t, slot, t_next, slot_next):
        phase = [0]

        def issue_some():
            if t_next is not None:
                issue(t_next, slot_next, phase[0] * per_phase, (phase[0] + 1) * per_phase)
            phase[0] += 1

        wait(slot)
        gt = g_ref[pl.ds(t, 1), :]
        xt = x_ref[pl.ds(t, 1), :]
        acc = chunk(slot, 0) * xt[:, :LANE]
        issue_some()
        for c in range(1, n_chunk):
            acc = acc + chunk(slot, c) * xt[:, c * LANE:(c + 1) * LANE]
            issue_some()
        act = jnp.sum(acc, axis=1, keepdims=True)
        g_col = jnp.sum(jnp.where(eye, gt, 0.0), axis=1, keepdims=True)
        w = g_col * (0.5 * act * (1.0 + lax.erf(act * np.float32(np.sqrt(0.5)))))
        outs = []
        for c in range(n_chunk):
            outs.append(jnp.sum(w * chunk(slot, n_chunk + c), axis=0, keepdims=True))
            issue_some()
        o_ref[pl.ds(t, 1), :] = jnp.concatenate(outs, axis=1)

    ahead = PEER_NBUF - 1
    for s in range(ahead):
        issue(s, s, 0, PEER_PAIRS)

    def group(base, n_issue):
        for j in range(PEER_NBUF):
            if j < n_issue:
                compute(base + j, j, base + j + ahead, (j + ahead) % PEER_NBUF)
            else:
                compute(base + j, j, None, None)

    n_group = n_tok // PEER_NBUF

    def steady(gi, carry):
        group(gi * PEER_NBUF, PEER_NBUF)
        return carry

    lax.fori_loop(0, n_group - 1, steady, 0)
    group((n_group - 1) * PEER_NBUF, PEER_NBUF - ahead)


def _peer_gather(x, idx, g, uv_tab):
    n, d = x.shape
    return pl.pallas_call(
        _peer_gather_kernel,
        grid=(n // PEER_TB,),
        in_specs=[
            pl.BlockSpec((PEER_TB, PEER_PAIRS), lambda i: (i, 0), memory_space=pltpu.SMEM),
            pl.BlockSpec((PEER_TB, d), lambda i: (i, 0)),
            pl.BlockSpec((PEER_TB, PEER_PAIRS), lambda i: (i, 0)),
            pl.BlockSpec(memory_space=pl.ANY),
        ],
        out_specs=pl.BlockSpec((PEER_TB, d), lambda i: (i, 0)),
        out_shape=jax.ShapeDtypeStruct((n, d), jnp.float32),
        scratch_shapes=[pltpu.VMEM((PEER_NBUF, PEER_PAIRS * SLAB_PITCH, LANE), jnp.float32), pltpu.SemaphoreType.DMA((PEER_NBUF,))],
        name="peer_gather",
    )(idx * SLAB_ROWS, x, g, uv_tab)


def _top_rows(s, k, payload=None):
    n_rows = s.shape[0]
    row = lax.broadcasted_iota(jnp.int32, s.shape, 0)
    out_row = lax.broadcasted_iota(jnp.int32, (k, s.shape[1]), 0)
    vals = jnp.zeros((k, s.shape[1]), jnp.float32)
    sel = jnp.zeros((k, s.shape[1]), jnp.int32)
    for j in range(k):
        m = jnp.max(s, axis=0, keepdims=True)
        r = jnp.min(jnp.where(s == m, row, n_rows), axis=0, keepdims=True)
        hit = row == r
        s = jnp.where(hit, -jnp.inf, s)
        what = r if payload is None else jnp.max(jnp.where(hit, payload, -1), axis=0, keepdims=True)
        vals = jnp.where(out_row == j, m, vals)
        sel = jnp.where(out_row == j, what, sel)
    return vals, sel


def _peer_select_kernel(x_ref, sc_ref, sh_ref, wq_ref, keys_ref, xm_ref, idx_ref, g_ref, q_scr, sv_scr, si_scr, bi_scr, bg_scr):
    f32 = jnp.float32
    n_hc = 2 * H_P
    h = x_ref[...] * (1.0 + sc_ref[0]) + sh_ref[0]
    xm_ref[...] = h
    q = jnp.dot(h.astype(jnp.bfloat16), wq_ref[...], preferred_element_type=f32)
    for hc in range(n_hc):
        q_scr[hc] = q[:, hc * D_HALF:(hc + 1) * D_HALF].astype(jnp.bfloat16)

    def sub_key_top(hc, carry):
        s = lax.dot_general(keys_ref[hc], q_scr[hc], (((1,), (1,)), ((), ())), preferred_element_type=f32)
        sv_scr[hc], si_scr[hc] = _top_rows(s, TOPK)
        return carry

    lax.fori_loop(0, n_hc, sub_key_top, 0)

    def pair_top(hd, carry):
        sv0, sv1 = sv_scr[2 * hd], sv_scr[2 * hd + 1]
        si0, si1 = si_scr[2 * hd], si_scr[2 * hd + 1]
        half = TOPK // 2
        cand = [sv0[0:1] + sv1]
        cidx = [si0[0:1] * N_KEYS + si1]
        for a in range(1, half):
            cand.append(sv0[a:a + 1] + sv1[:half])
            cidx.append(si0[a:a + 1] * N_KEYS + si1[:half])
        cand.append(sv0[half:] + sv1[0:1])
        cidx.append(si0[half:] * N_KEYS + si1[0:1])
        best, eid = _top_rows(jnp.concatenate(cand, axis=0), TOPK, payload=jnp.concatenate(cidx, axis=0))
        e = jnp.exp(best - best[0:1])
        bi_scr[hd] = eid
        bg_scr[hd] = e / jnp.sum(e, axis=0, keepdims=True)
        return carry

    lax.fori_loop(0, H_P, pair_top, 0)
    tm = x_ref.shape[0]
    idx_ref[...] = bi_scr[...].reshape(PEER_PAIRS, tm).T
    g_ref[...] = bg_scr[...].reshape(PEER_PAIRS, tm).T


def _peer_select(x, sc, sh, w_q, sub_keys):
    n_seq, seq_len, d = x.shape
    n = n_seq * seq_len
    n_hc = 2 * H_P
    per_seq = seq_len // SEL_TM
    tok = lambda i: (i, 0)
    seq = lambda i: (i // per_seq, 0, 0)
    return pl.pallas_call(
        _peer_select_kernel,
        grid=(n // SEL_TM,),
        in_specs=[
            pl.BlockSpec((SEL_TM, d), tok),
            pl.BlockSpec((1, 1, d), seq),
            pl.BlockSpec((1, 1, d), seq),
            pl.BlockSpec((d, H_P * D_QK), lambda i: (0, 0)),
            pl.BlockSpec((n_hc, N_KEYS, D_HALF), lambda i: (0, 0, 0)),
        ],
        out_specs=[pl.BlockSpec((SEL_TM, d), tok), pl.BlockSpec((SEL_TM, PEER_PAIRS), tok), pl.BlockSpec((SEL_TM, PEER_PAIRS), tok)],
        out_shape=[jax.ShapeDtypeStruct((n, d), jnp.float32), jax.ShapeDtypeStruct((n, PEER_PAIRS), jnp.int32), jax.ShapeDtypeStruct((n, PEER_PAIRS), jnp.float32)],
        scratch_shapes=[
            pltpu.VMEM((n_hc, SEL_TM, D_HALF), jnp.bfloat16),
            pltpu.VMEM((n_hc, TOPK, SEL_TM), jnp.float32),
            pltpu.VMEM((n_hc, TOPK, SEL_TM), jnp.int32),
            pltpu.VMEM((H_P, TOPK, SEL_TM), jnp.int32),
            pltpu.VMEM((H_P, TOPK, SEL_TM), jnp.float32),
        ],
        name="peer_select",
    )(x.reshape(n, d), sc, sh, w_q.astype(jnp.bfloat16), sub_keys.reshape(n_hc, N_KEYS, D_HALF).astype(jnp.bfloat16))


def _residual_norm_kernel(x_ref, f_ref, g_ref, lng_ref, lnb_ref, o_ref):
    o_ref[...] = _layer_norm_rows(ALPHA * x_ref[...] + g_ref[0] * f_ref[...], lng_ref[...], lnb_ref[...])


def _residual_norm(x, f, gate, ln_g, ln_b):
    n_seq, seq_len, d = x.shape
    n = n_seq * seq_len
    per_seq = seq_len // ROW_TILE
    row = pl.BlockSpec((ROW_TILE, d), lambda i: (i, 0))
    vec = pl.BlockSpec((1, d), lambda i: (0, 0))
    return pl.pallas_call(
        _residual_norm_kernel,
        grid=(n // ROW_TILE,),
        in_specs=[row, row, pl.BlockSpec((1, 1, d), lambda i: (i // per_seq, 0, 0)), vec, vec],
        out_specs=row,
        out_shape=jax.ShapeDtypeStruct((n, d), jnp.float32),
        name="residual_norm",
    )(x.reshape(n, d), f.reshape(n, d), gate, ln_g.reshape(1, d), ln_b.reshape(1, d)).reshape(x.shape)


def _peer(x, sc, sh, w_q, sub_keys, uv_tab):
    xm, idx, g = _peer_select(x, sc, sh, w_q, sub_keys)
    return _peer_gather(xm, idx, g, uv_tab).reshape(x.shape)


def kernel(x_prompt, x_sample, c, c_ctx, state_mlstm_C, state_mlstm_n, state_mlstm_m, state_ret, state_ssd, w_mod, b_mod, ln1_g, ln1_b, ln2_g, ln2_b, even_w_in, mlstm_i_bias, mlstm_f_bias, ret_log_decay, even_gn_a, even_gn_b, even_w_out, odd_w_in, odd_conv_w, odd_conv_b, odd_dt_bias, odd_a_log, odd_d, odd_norm_w, odd_w_out, peer_w_q, peer_sub_keys, peer_u, peer_v):
    f32 = jnp.float32
    bp = x_prompt.shape[0]
    xp = x_prompt
    xs = x_sample + _grid_pos_embed(x_sample.shape[1], x_sample.dtype)[None]
    new_C, new_n, new_m, new_S, new_h = [], [], [], [], []
    bcast = lambda m: jnp.broadcast_to(m, (bp, 1, D_MODEL))
    for l in range(DEPTH):
        sh1p, sc1p, g1p, sh2p, sc2p, g2p = [bcast(m) for m in _modulation(c_ctx[None, :], w_mod[l], b_mod[l])]
        sh1s, sc1s, g1s, sh2s, sc2s, g2s = _modulation(c, w_mod[l], b_mod[l])
        j = l // 2
        if l % 2 == 0:
            wts = _even_weights(even_w_in[j], mlstm_i_bias[j], mlstm_f_bias[j], ret_log_decay[j], even_gn_a[j], even_gn_b[j], even_w_out[j])
            zero = (jnp.zeros((bp, N_DIR, H_A, DK_A, DV_A), f32), jnp.zeros((bp, N_DIR, H_A, DK_A), f32), jnp.zeros((bp, N_DIR, H_A), f32), jnp.zeros((bp, N_DIR, H_B, DK_B, DV_B), f32))
            xp, fin = _even_layer(xp, sc1p, sh1p, g1p, wts, ln1_g[l], ln1_b[l], *zero)
            xs, _ = _even_layer(xs, sc1s, sh1s, g1s, wts, ln1_g[l], ln1_b[l], state_mlstm_C[:, j], state_mlstm_n[:, j], state_mlstm_m[:, j], state_ret[:, j])
            new_C.append(fin[0])
            new_n.append(fin[1])
            new_m.append(fin[2])
            new_S.append(fin[3])
        else:
            wts = _odd_weights(odd_w_in[j], odd_dt_bias[j], odd_a_log[j], odd_d[j], odd_norm_w[j], odd_w_out[j])
            xp, fin = _odd_layer(xp, sc1p, sh1p, g1p, wts, odd_conv_w[j], odd_conv_b[j], ln1_g[l], ln1_b[l], jnp.zeros((bp, N_DIR, H_C, P_C, N_C), f32))
            xs, _ = _odd_layer(xs, sc1s, sh1s, g1s, wts, odd_conv_w[j], odd_conv_b[j], ln1_g[l], ln1_b[l], state_ssd[:, j])
            new_h.append(fin)
        uv_tab = jnp.concatenate([peer_u[l], peer_v[l]], axis=1).reshape(-1, LANE)
        fp = _peer(xp, sc2p, sh2p, peer_w_q[l], peer_sub_keys[l], uv_tab)
        fs = _peer(xs, sc2s, sh2s, peer_w_q[l], peer_sub_keys[l], uv_tab)
        xp = _residual_norm(xp, fp, g2p, ln2_g[l], ln2_b[l])
        xs = _residual_norm(xs, fs, g2s, ln2_g[l], ln2_b[l])
    return (xp, xs, jnp.stack(new_C, axis=1), jnp.stack(new_n, axis=1), jnp.stack(new_m, axis=1), jnp.stack(new_S, axis=1), jnp.stack(new_h, axis=1))
```

```python
import functools

import jax
import jax.numpy as jnp
import numpy as np
from jax import lax
from jax.experimental import pallas as pl
from jax.experimental.pallas import tpu as pltpu

D_MODEL = 1024
DEPTH = 4
GRID_W = 64
CHUNK = 64
N_DIR = 2
ALPHA = (2.0 * DEPTH) ** 0.25
H_A = 4
DK_A = 128
DV_A = 128
H_B = 4
DK_B = 128
DV_B = 128
W_A = H_A * DV_A
W_B = H_B * DV_B
EVEN_SIZES = (H_A * DK_A, H_A * DK_A, W_A, W_A, N_DIR * H_A, N_DIR * H_A, H_B * DK_B, H_B * DK_B, W_B, W_B)
EVEN_SPLITS = [int(s) for s in np.cumsum(EVEN_SIZES)[:-1]]
D_INNER = 2 * D_MODEL
P_C = 64
H_C = D_INNER // P_C
G_C = 4
J_C = H_C // G_C
N_C = 128
D_CONV = 5
CONV_DIM = D_INNER + 2 * G_C * N_C
N_KEYS = 128
H_P = 8
D_QK = 256
D_HALF = D_QK // 2
TOPK = 16

LANE = 128
ROW_TILE = 256
PEER_PAIRS = H_P * TOPK
SEL_TM = 256
PEER_TB = 256
PEER_NBUF = 8
SLAB_ROWS = 2 * D_MODEL // LANE
SLAB_PITCH = SLAB_ROWS + 4


def _mod_proj_kernel(x_ref, sc_ref, sh_ref, w_ref, o_ref):
    h = x_ref[...] * (1.0 + sc_ref[0]) + sh_ref[0]
    o_ref[...] = jnp.dot(h.astype(jnp.bfloat16), w_ref[...], preferred_element_type=jnp.float32)


def _mod_proj(x, sc, sh, wb):
    n_seq, seq_len, d = x.shape
    n = n_seq * seq_len
    n_out = wb.shape[1]
    per_seq = seq_len // ROW_TILE
    seq = lambda i: (i // per_seq, 0, 0)
    return pl.pallas_call(
        _mod_proj_kernel,
        grid=(n // ROW_TILE,),
        in_specs=[pl.BlockSpec((ROW_TILE, d), lambda i: (i, 0)), pl.BlockSpec((1, 1, d), seq), pl.BlockSpec((1, 1, d), seq), pl.BlockSpec((d, n_out), lambda i: (0, 0))],
        out_specs=pl.BlockSpec((ROW_TILE, n_out), lambda i: (i, 0)),
        out_shape=jax.ShapeDtypeStruct((n, n_out), jnp.float32),
        name="mod_proj",
    )(x.reshape(n, d), sc, sh, wb)


def _layer_norm_rows(y, g, b, eps=1e-5):
    mu = jnp.mean(y, axis=-1, keepdims=True)
    yc = y - mu
    var = jnp.mean(yc * yc, axis=-1, keepdims=True)
    return yc * lax.rsqrt(var + eps) * g + b


EVEN_MAIN = 8 * W_A
EVEN_COLS = EVEN_MAIN + LANE
N_HD = N_DIR * H_A


def _even_scan_kernel(pf_ref, pb_ref, c0_ref, n0_ref, m0_ref, s0_ref, gb_ref, lg_ref,
                      hf_ref, hb_ref, cf_ref, nf_ref, mf_ref, sf_ref, c_scr, n_scr, m_scr, s_scr):
    f32, bf16 = jnp.float32, jnp.bfloat16
    ci = pl.program_id(1)

    @pl.when(ci == 0)
    def _():
        c_scr[...] = c0_ref[0]
        n_scr[...] = n0_ref[0]
        m_scr[...] = m0_ref[0]
        s_scr[...] = s0_ref[0]

    t_i = lax.broadcasted_iota(jnp.int32, (CHUNK, CHUNK), 0)
    s_i = lax.broadcasted_iota(jnp.int32, (CHUNK, CHUNK), 1)
    t_col = lax.broadcasted_iota(jnp.int32, (CHUNK, 1), 0).astype(f32)
    nt = (((1,), (1,)), ((), ()))
    tn = (((0,), (0,)), ((), ()))
    hi = lax.Precision.HIGHEST

    for d, (p_ref, h_ref) in enumerate(((pf_ref, hf_ref), (pb_ref, hb_ref))):
        mask = (s_i <= t_i) if d == 0 else (s_i >= t_i)
        tri = mask.astype(f32)
        last = CHUNK - 1 if d == 0 else 0
        gates = p_ref[:, EVEN_MAIN:] + gb_ref[...]
        lf = jax.nn.log_sigmoid(gates)
        b_cols = jnp.dot(tri, lf, precision=hi, preferred_element_type=f32)
        b_rows = lax.dot_general(lf.T, tri, nt, precision=hi, preferred_element_type=f32)
        li_rows = gates.T
        diff = (t_i - s_i if d == 0 else s_i - t_i).astype(f32)
        tau = t_col if d == 0 else (CHUNK - 1.0) - t_col
        base = 4 * W_A
        qs, ks, vs, qk, qst = [], [], [], [], []
        for hd in range(H_A):
            j = d * H_A + hd
            q = p_ref[:, hd * DK_A:(hd + 1) * DK_A]
            k = p_ref[:, W_A + hd * DK_A:W_A + (hd + 1) * DK_A] * DK_A ** -0.5
            vb_ = p_ref[:, 2 * W_A + hd * DV_A:2 * W_A + (hd + 1) * DV_A].astype(bf16)
            qb_ = q.astype(bf16)
            qs.append(q), ks.append(k), vs.append(vb_)
            qk.append(lax.dot_general(qb_, k.astype(bf16), nt, preferred_element_type=f32))
            qst.append(jnp.dot(qb_, c_scr[j].astype(bf16), preferred_element_type=f32))
        for hd in range(H_B):
            j = d * H_B + hd
            q = p_ref[:, base + hd * DK_B:base + (hd + 1) * DK_B] * DK_B ** -0.5
            k = p_ref[:, base + W_B + hd * DK_B:base + W_B + (hd + 1) * DK_B]
            vb_ = p_ref[:, base + 2 * W_B + hd * DV_B:base + 2 * W_B + (hd + 1) * DV_B].astype(bf16)
            qb_ = q.astype(bf16)
            qs.append(q), ks.append(k), vs.append(vb_)
            qk.append(lax.dot_general(qb_, k.astype(bf16), nt, preferred_element_type=f32))
            qst.append(jnp.dot(qb_, s_scr[j].astype(bf16), preferred_element_type=f32))
        s_all, kw, post = [], [], []
        for hd in range(H_A):
            j = d * H_A + hd
            b_col = b_cols[:, N_HD + j:N_HD + j + 1]
            b_row = b_rows[N_HD + j:N_HD + j + 1, :]
            li_col = gates[:, j:j + 1]
            li_row = li_rows[j:j + 1, :]
            m_prev = m_scr[j][:, 0:1]
            dlog = jnp.where(mask, b_col - b_row + li_row, -jnp.inf)
            inter = b_col + m_prev
            m_t = jnp.maximum(inter, jnp.max(dlog, axis=1, keepdims=True))
            s = qk[hd] * jnp.exp(dlog - m_t)
            w_int = jnp.exp(inter - m_t)
            den = jnp.sum(s, axis=1, keepdims=True) + w_int * jnp.sum(qs[hd] * n_scr[j], axis=1, keepdims=True)
            b_last = b_col[last:last + 1]
            g_col = b_last - b_col + li_col
            m_new = jnp.maximum(b_last + m_prev, jnp.max(g_col, axis=0, keepdims=True))
            wg = jnp.exp(g_col - m_new)
            dec = jnp.exp(b_last + m_prev - m_new)
            s_all.append(s.astype(bf16))
            kw.append(ks[hd] * wg)
            post.append((w_int, jnp.maximum(jnp.abs(den), jnp.exp(-m_t)), dec, m_new))
        for hd in range(H_B):
            lg = lg_ref[d, hd]
            decay = jnp.where(diff >= 0, jnp.exp(jnp.maximum(diff, 0.0) * lg), 0.0)
            xi = jnp.exp((tau + 1.0) * lg)
            zeta = jnp.exp((CHUNK - 1.0 - tau) * lg)
            g_end = jnp.exp(jnp.full((1, 1), CHUNK, f32) * lg)
            s_all.append((qk[H_A + hd] * decay).astype(bf16))
            kw.append(ks[H_A + hd] * zeta)
            post.append((xi, g_end))
        for hd in range(H_A):
            j = d * H_A + hd
            w_int, den, dec, m_new = post[hd]
            num = jnp.dot(s_all[hd], vs[hd], preferred_element_type=f32) + w_int * qst[hd]
            h_ref[:, hd * DV_A:(hd + 1) * DV_A] = num / den
            c_scr[j] = dec * c_scr[j] + lax.dot_general(kw[hd].astype(bf16), vs[hd], tn, preferred_element_type=f32)
            n_scr[j] = dec * n_scr[j] + jnp.sum(kw[hd], axis=0, keepdims=True)
            m_scr[j] = jnp.broadcast_to(m_new, (1, DK_A))
        for hd in range(H_B):
            j = d * H_B + hd
            xi, g_end = post[H_A + hd]
            o = jnp.dot(s_all[H_A + hd], vs[H_A + hd], preferred_element_type=f32) + xi * qst[H_A + hd]
            h_ref[:, W_A + hd * DV_B:W_A + (hd + 1) * DV_B] = o
            s_scr[j] = g_end * s_scr[j] + lax.dot_general(kw[H_A + hd].astype(bf16), vs[H_A + hd], tn, preferred_element_type=f32)

    @pl.when(ci == pl.num_programs(1) - 1)
    def _():
        cf_ref[0] = c_scr[...]
        nf_ref[0] = n_scr[...]
        mf_ref[0] = m_scr[...]
        sf_ref[0] = s_scr[...]


def _even_scan(proj, n_seq, seq_len, c0, n0, m0, s0, gate_bias, log_gamma):
    nc = seq_len // CHUNK
    f32 = jnp.float32
    st4 = lambda s, c: (s, 0, 0, 0)
    big = pl.BlockSpec((1, N_HD, DK_A, DV_A), st4)
    small = pl.BlockSpec((1, N_HD, 1, DK_A), st4)
    fwd = lambda s, c: (s * nc + c, 0)
    bwd = lambda s, c: (s * nc + (nc - 1 - c), 0)
    n = n_seq * seq_len
    return pl.pallas_call(
        _even_scan_kernel,
        grid=(n_seq, nc),
        in_specs=[pl.BlockSpec((CHUNK, EVEN_COLS), fwd), pl.BlockSpec((CHUNK, EVEN_COLS), bwd), big, small, small, big,
                  pl.BlockSpec((1, LANE), lambda s, c: (0, 0)), pl.BlockSpec(memory_space=pltpu.SMEM)],
        out_specs=[pl.BlockSpec((CHUNK, W_A + W_B), fwd), pl.BlockSpec((CHUNK, W_A + W_B), bwd), big, small, small, big],
        out_shape=[jax.ShapeDtypeStruct((n, W_A + W_B), f32), jax.ShapeDtypeStruct((n, W_A + W_B), f32),
                   jax.ShapeDtypeStruct(c0.shape, f32), jax.ShapeDtypeStruct(n0.shape, f32), jax.ShapeDtypeStruct(m0.shape, f32), jax.ShapeDtypeStruct(s0.shape, f32)],
        scratch_shapes=[pltpu.VMEM((N_HD, DK_A, DV_A), f32), pltpu.VMEM((N_HD, 1, DK_A), f32), pltpu.VMEM((N_HD, 1, DK_A), f32), pltpu.VMEM((N_HD, DK_B, DV_B), f32)],
        name="even_scan",
    )(proj, proj, c0, n0, m0, s0, gate_bias, log_gamma)


def _even_post_kernel(hf_ref, hb_ref, oa_ref, gb_ref, x_ref, g1_ref, gn_ref, w_ref, lng_ref, lnb_ref, o_ref):
    hs = hf_ref[...] + hb_ref[...]
    parts = []
    for hd in range(H_A + H_B):
        hh = hs[:, hd * DV_A:(hd + 1) * DV_A]
        mu = jnp.mean(hh, axis=-1, keepdims=True)
        hc = hh - mu
        var = jnp.mean(hc * hc, axis=-1, keepdims=True)
        parts.append(hc * lax.rsqrt(var + 1e-5))
    normed = jnp.concatenate(parts, axis=1) * gn_ref[...]
    gb = gb_ref[...]
    act = jnp.concatenate([jax.nn.sigmoid(oa_ref[...]), gb * jax.nn.sigmoid(gb)], axis=1)
    ms = jnp.dot((normed * act).astype(jnp.bfloat16), w_ref[...], preferred_element_type=jnp.float32)
    o_ref[...] = _layer_norm_rows(ALPHA * x_ref[...] + g1_ref[0] * ms, lng_ref[...], lnb_ref[...])


def _even_post(hf, hb, proj, x, g1, gn, w_out_b, ln_g, ln_b):
    n_seq, seq_len, d = x.shape
    n = n_seq * seq_len
    per_seq = seq_len // ROW_TILE
    row = lambda i: (i, 0)
    full = lambda i: (0, 0)
    return pl.pallas_call(
        _even_post_kernel,
        grid=(n // ROW_TILE,),
        in_specs=[pl.BlockSpec((ROW_TILE, W_A + W_B), row), pl.BlockSpec((ROW_TILE, W_A + W_B), row),
                  pl.BlockSpec((ROW_TILE, W_A), lambda i: (i, 3)), pl.BlockSpec((ROW_TILE, W_B), lambda i: (i, 7)),
                  pl.BlockSpec((ROW_TILE, d), row), pl.BlockSpec((1, 1, d), lambda i: (i // per_seq, 0, 0)),
                  pl.BlockSpec((1, W_A + W_B), full), pl.BlockSpec((W_A + W_B, d), full), pl.BlockSpec((1, d), full), pl.BlockSpec((1, d), full)],
        out_specs=pl.BlockSpec((ROW_TILE, d), row),
        out_shape=jax.ShapeDtypeStruct((n, d), jnp.float32),
        name="even_post",
    )(hf, hb, proj, proj, x.reshape(n, d), g1, gn, w_out_b, ln_g, ln_b).reshape(x.shape)


def _even_weights(w_in, i_bias, f_bias, log_decay, gn_a, gn_b, w_out):
    qa, ka, va, oa, ia, fa, qb, kb, vb, gb = jnp.split(w_in, EVEN_SPLITS, axis=1)
    gates = jnp.pad(jnp.concatenate([ia, fa], axis=1), ((0, 0), (0, LANE - 2 * N_HD)))
    w_in_b = jnp.concatenate([qa, ka, va, oa, qb, kb, vb, gb, gates], axis=1).astype(jnp.bfloat16)
    gate_bias = jnp.pad(jnp.concatenate([i_bias.reshape(-1), f_bias.reshape(-1)]), (0, LANE - 2 * N_HD)).reshape(1, LANE)
    log_gamma = -jnp.exp(log_decay.astype(jnp.float32))
    gn = jnp.concatenate([gn_a.reshape(-1), gn_b.reshape(-1)]).reshape(1, W_A + W_B)
    return w_in_b, gate_bias, log_gamma, gn, w_out.astype(jnp.bfloat16)


def _even_layer(x, sc1, sh1, g1, wts, ln_g, ln_b, c0, n0, m0, s0):
    w_in_b, gate_bias, log_gamma, gn, w_out_b = wts
    n_seq, seq_len, d = x.shape
    proj = _mod_proj(x, sc1, sh1, w_in_b)
    st = lambda a: a.reshape(n_seq, N_HD, DK_A, DV_A)
    vec = lambda a: a.reshape(n_seq, N_HD, 1, DK_A)
    m0b = jnp.broadcast_to(m0.reshape(n_seq, N_HD, 1, 1), (n_seq, N_HD, 1, DK_A))
    hf, hb, cf, nf, mf, sf = _even_scan(proj, n_seq, seq_len, st(c0), vec(n0), m0b, st(s0), gate_bias, log_gamma)
    x_new = _even_post(hf, hb, proj, x, g1, gn, w_out_b, ln_g.reshape(1, d), ln_b.reshape(1, d))
    fin = (cf.reshape(n_seq, N_DIR, H_A, DK_A, DV_A), nf.reshape(n_seq, N_DIR, H_A, DK_A), mf[:, :, 0, 0].reshape(n_seq, N_DIR, H_A), sf.reshape(n_seq, N_DIR, H_B, DK_B, DV_B))
    return x_new, fin


ODD_DT = D_INNER + CONV_DIM
ODD_COLS = ODD_DT + LANE
CONV_TILE = 1024
HALO = 8
GW = J_C * P_C


def _conv_kernel(x_ref, prev_ref, next_ref, w_ref, b_ref, o_ref, *, per_seq):
    i = pl.program_id(0)
    pad = D_CONV // 2
    rows = x_ref.shape[0]
    first = lax.rem(i, per_seq) == 0
    last = lax.rem(i, per_seq) == per_seq - 1
    prev = jnp.where(first, 0.0, prev_ref[HALO - pad:, :])
    nxt = jnp.where(last, 0.0, next_ref[:pad, :])
    ext = jnp.concatenate([prev, x_ref[...], nxt], axis=0)
    acc = b_ref[...] + ext[0:rows] * w_ref[0:1, :]
    for k in range(1, D_CONV):
        acc = acc + ext[k:k + rows] * w_ref[k:k + 1, :]
    o_ref[...] = acc * jax.nn.sigmoid(acc)


def _conv_silu(proj, seq_len, conv_w, conv_b):
    n = proj.shape[0]
    per_seq = seq_len // ROW_TILE
    c0 = D_INNER // CONV_TILE
    hb = ROW_TILE // HALO
    n_hb = n // HALO
    return pl.pallas_call(
        functools.partial(_conv_kernel, per_seq=per_seq),
        grid=(n // ROW_TILE, CONV_DIM // CONV_TILE),
        in_specs=[pl.BlockSpec((ROW_TILE, CONV_TILE), lambda i, j: (i, c0 + j)),
                  pl.BlockSpec((HALO, CONV_TILE), lambda i, j: (jnp.maximum(i * hb - 1, 0), c0 + j)),
                  pl.BlockSpec((HALO, CONV_TILE), lambda i, j: (jnp.minimum((i + 1) * hb, n_hb - 1), c0 + j)),
                  pl.BlockSpec((D_CONV, CONV_TILE), lambda i, j: (0, j)), pl.BlockSpec((1, CONV_TILE), lambda i, j: (0, j))],
        out_specs=pl.BlockSpec((ROW_TILE, CONV_TILE), lambda i, j: (i, j)),
        out_shape=jax.ShapeDtypeStruct((n, CONV_DIM), jnp.float32),
        name="conv_silu",
    )(proj, proj, proj, conv_w, conv_b.reshape(1, CONV_DIM))


def _spread_heads(mat, col0):
    lane = lax.broadcasted_iota(jnp.int32, (mat.shape[0], LANE), 1)
    per_tile = LANE // P_C
    tiles = []
    for j in range(0, J_C, per_tile):
        t = mat[:, col0 + j:col0 + j + 1]
        for r in range(1, per_tile):
            t = jnp.where(lane < r * P_C, t, mat[:, col0 + j + r:col0 + j + r + 1])
        tiles.append(jnp.broadcast_to(t, (mat.shape[0], LANE)))
    return jnp.concatenate(tiles, axis=1)


def _ssd_scan_kernel(af_ref, ab_ref, dtf_ref, dtb_ref, bias_ref, alog_ref, h0_ref, yf_ref, yb_ref, hf_ref, h_scr):
    f32, bf16 = jnp.float32, jnp.bfloat16
    ci = pl.program_id(1)

    @pl.when(ci == 0)
    def _():
        h_scr[...] = h0_ref[0]

    t_i = lax.broadcasted_iota(jnp.int32, (CHUNK, CHUNK), 0)
    s_i = lax.broadcasted_iota(jnp.int32, (CHUNK, CHUNK), 1)
    nt = (((1,), (1,)), ((), ()))
    tn = (((0,), (0,)), ((), ()))
    hi = lax.Precision.HIGHEST
    a_neg = -jnp.exp(alog_ref[...])

    for d, (x_ref, dt_ref, y_ref) in enumerate(((af_ref, dtf_ref, yf_ref), (ab_ref, dtb_ref, yb_ref))):
        mask = (s_i <= t_i) if d == 0 else (s_i >= t_i)
        tri = mask.astype(f32)
        last = CHUNK - 1 if d == 0 else 0
        dtv = jax.nn.softplus(dt_ref[...] + bias_ref[...])
        a = dtv * a_neg
        ac_cols = jnp.dot(tri, a, precision=hi, preferred_element_type=f32)
        ac_rows = lax.dot_general(a.T, tri, nt, precision=hi, preferred_element_type=f32)
        dt_rows = dtv.T
        e_cols = jnp.exp(ac_cols)
        w_cols = jnp.exp(ac_cols[last:last + 1] - ac_cols) * dtv
        bgs, xgs, cbs, inters = [], [], [], []
        for g in range(G_C):
            bg = x_ref[:, D_INNER + g * N_C:D_INNER + (g + 1) * N_C].astype(bf16)
            cg = x_ref[:, D_INNER + G_C * N_C + g * N_C:D_INNER + G_C * N_C + (g + 1) * N_C].astype(bf16)
            bgs.append(bg)
            xgs.append(x_ref[:, g * GW:(g + 1) * GW])
            cbs.append(lax.dot_general(cg, bg, nt, preferred_element_type=f32))
            inters.append(jnp.dot(cg, h_scr[d, g].astype(bf16), preferred_element_type=f32))
        s_all = []
        for g in range(G_C):
            for j in range(J_C):
                col = d * H_C + g * J_C + j
                seg = jnp.where(mask, ac_cols[:, col:col + 1] - ac_rows[col:col + 1, :], -jnp.inf)
                s_all.append((cbs[g] * jnp.exp(seg) * dt_rows[col:col + 1, :]).astype(bf16))
        for g in range(G_C):
            col0 = d * H_C + g * J_C
            e_g = _spread_heads(e_cols, col0)
            intra = [jnp.dot(s_all[g * J_C + j], xgs[g][:, j * P_C:(j + 1) * P_C].astype(bf16), preferred_element_type=f32) for j in range(J_C)]
            y_ref[:, g * GW:(g + 1) * GW] = jnp.concatenate(intra, axis=1) + e_g * inters[g]
            xw = (xgs[g] * _spread_heads(w_cols, col0)).astype(bf16)
            h_scr[d, g] = e_g[last:last + 1] * h_scr[d, g] + lax.dot_general(bgs[g], xw, tn, preferred_element_type=f32)

    @pl.when(ci == pl.num_programs(1) - 1)
    def _():
        hf_ref[0] = h_scr[...]


def _ssd_chunk_scan(xa, proj, n_seq, seq_len, dt_bias_row, a_log_row, h0):
    nc = seq_len // CHUNK
    n = n_seq * seq_len
    f32 = jnp.float32
    fwd = lambda s, c: (s * nc + c, 0)
    bwd = lambda s, c: (s * nc + (nc - 1 - c), 0)
    dcol = ODD_DT // LANE
    st = pl.BlockSpec((1, N_DIR, G_C, N_C, GW), lambda s, c: (s, 0, 0, 0, 0))
    row = pl.BlockSpec((1, LANE), lambda s, c: (0, 0))
    return pl.pallas_call(
        _ssd_scan_kernel,
        grid=(n_seq, nc),
        in_specs=[pl.BlockSpec((CHUNK, CONV_DIM), fwd), pl.BlockSpec((CHUNK, CONV_DIM), bwd),
                  pl.BlockSpec((CHUNK, LANE), lambda s, c: (s * nc + c, dcol)), pl.BlockSpec((CHUNK, LANE), lambda s, c: (s * nc + (nc - 1 - c), dcol)),
                  row, row, st],
        out_specs=[pl.BlockSpec((CHUNK, D_INNER), fwd), pl.BlockSpec((CHUNK, D_INNER), bwd), st],
        out_shape=[jax.ShapeDtypeStruct((n, D_INNER), f32), jax.ShapeDtypeStruct((n, D_INNER), f32), jax.ShapeDtypeStruct(h0.shape, f32)],
        scratch_shapes=[pltpu.VMEM((N_DIR, G_C, N_C, GW), f32)],
        name="ssd_scan",
    )(xa, xa, proj, proj, dt_bias_row, a_log_row, h0)


def _odd_post_kernel(yf_ref, yb_ref, xs_ref, z_ref, x_ref, g1_ref, dsk_ref, nw_ref, w_ref, lng_ref, lnb_ref, o_ref):
    z = z_ref[...]
    y = (yf_ref[...] + yb_ref[...] + dsk_ref[...] * xs_ref[...]) * (z * jax.nn.sigmoid(z))
    y = y * lax.rsqrt(jnp.mean(y * y, axis=-1, keepdims=True) + 1e-5) * nw_ref[...]
    ms = jnp.dot(y.astype(jnp.bfloat16), w_ref[...], preferred_element_type=jnp.float32)
    o_ref[...] = _layer_norm_rows(ALPHA * x_ref[...] + g1_ref[0] * ms, lng_ref[...], lnb_ref[...])


def _odd_post(yf, yb, xa, proj, x, g1, d_row, norm_w, w_out_b, ln_g, ln_b):
    n_seq, seq_len, d = x.shape
    n = n_seq * seq_len
    per_seq = seq_len // ROW_TILE
    row = lambda i: (i, 0)
    full = lambda i: (0, 0)
    wide = pl.BlockSpec((ROW_TILE, D_INNER), row)
    return pl.pallas_call(
        _odd_post_kernel,
        grid=(n // ROW_TILE,),
        in_specs=[wide, wide, wide, wide, pl.BlockSpec((ROW_TILE, d), row), pl.BlockSpec((1, 1, d), lambda i: (i // per_seq, 0, 0)),
                  pl.BlockSpec((1, D_INNER), full), pl.BlockSpec((1, D_INNER), full), pl.BlockSpec((D_INNER, d), full), pl.BlockSpec((1, d), full), pl.BlockSpec((1, d), full)],
        out_specs=pl.BlockSpec((ROW_TILE, d), row),
        out_shape=jax.ShapeDtypeStruct((n, d), jnp.float32),
        name="odd_post",
    )(yf, yb, xa, proj, x.reshape(n, d), g1, d_row, norm_w, w_out_b, ln_g, ln_b).reshape(x.shape)


def _odd_weights(w_in, dt_bias, a_log, d_skip, norm_w, w_out):
    w_in_b = jnp.pad(w_in, ((0, 0), (0, ODD_COLS - w_in.shape[1]))).astype(jnp.bfloat16)
    lane_row = lambda a: jnp.pad(a.astype(jnp.float32).reshape(-1), (0, LANE - N_DIR * H_C)).reshape(1, LANE)
    d_row = jnp.repeat(d_skip.astype(jnp.float32), P_C).reshape(1, D_INNER)
    return w_in_b, lane_row(dt_bias), lane_row(a_log), d_row, norm_w.reshape(1, D_INNER), w_out.astype(jnp.bfloat16)


def _odd_layer(x, sc1, sh1, g1, wts, conv_w, conv_b, ln_g, ln_b, h0):
    w_in_b, dt_row, alog_row, d_row, nw_row, w_out_b = wts
    n_seq, seq_len, d = x.shape
    proj = _mod_proj(x, sc1, sh1, w_in_b)
    xa = _conv_silu(proj, seq_len, conv_w, conv_b)
    to_scan = lambda h: h.reshape(n_seq, N_DIR, G_C, J_C, P_C, N_C).transpose(0, 1, 2, 5, 3, 4).reshape(n_seq, N_DIR, G_C, N_C, GW)
    yf, yb, hfin = _ssd_chunk_scan(xa, proj, n_seq, seq_len, dt_row, alog_row, to_scan(h0))
    x_new = _odd_post(yf, yb, xa, proj, x, g1, d_row, nw_row, w_out_b, ln_g.reshape(1, d), ln_b.reshape(1, d))
    fin = hfin.reshape(n_seq, N_DIR, G_C, N_C, J_C, P_C).transpose(0, 1, 2, 4, 5, 3).reshape(n_seq, N_DIR, H_C, P_C, N_C)
    return x_new, fin


def _modulation(cvec, w, b):
    m = jax.nn.silu(cvec) @ w + b
    return [t[:, None, :] for t in jnp.split(m, 6, axis=-1)]


def _grid_pos_embed(n_tok, dtype):
    rows = n_tok // GRID_W
    r = jnp.repeat(jnp.arange(rows, dtype=jnp.float32), GRID_W)
    col = jnp.tile(jnp.arange(GRID_W, dtype=jnp.float32), rows)
    quarter = D_MODEL // 4
    freqs = 1.0 / (10000.0 ** (jnp.arange(quarter, dtype=jnp.float32) / quarter))
    er = r[:, None] * freqs
    ec = col[:, None] * freqs
    return jnp.concatenate([jnp.sin(er), jnp.cos(er), jnp.sin(ec), jnp.cos(ec)], axis=-1).astype(dtype)


def _expert_slab_copy(tab_ref, buf_ref, sem_ref, row, pair, slot):
    src = tab_ref.at[pl.ds(pl.multiple_of(row, SLAB_ROWS), SLAB_ROWS), :]
    return pltpu.make_async_copy(src, buf_ref.at[slot, pl.ds(pair * SLAB_PITCH, SLAB_ROWS), :], sem_ref.at[slot])


def _peer_gather_kernel(idx_ref, x_ref, g_ref, tab_ref, o_ref, buf_ref, sem_ref):
    n_tok = x_ref.shape[0]
    n_chunk = x_ref.shape[1] // LANE

    per_phase = PEER_PAIRS // (2 * n_chunk)

    def issue(t, slot, p0, p1):
        for p in range(p0, p1):
            _expert_slab_copy(tab_ref, buf_ref, sem_ref, idx_ref[t, p], p, slot).start(priority=p % 2)

    def wait(slot):
        n_rows = PEER_PAIRS * SLAB_ROWS
        pltpu.make_async_copy(tab_ref.at[pl.ds(0, n_rows), :], buf_ref.at[slot, pl.ds(0, n_rows), :], sem_ref.at[slot]).wait()

    def chunk(slot, c):
        return buf_ref[slot, pl.ds(c, PEER_PAIRS, stride=SLAB_PITCH), :]

    eye = lax.broadcasted_iota(jnp.int32, (PEER_PAIRS, PEER_PAIRS), 0) == lax.broadcasted_iota(jnp.int32, (PEER_PAIRS, PEER_PAIRS), 1)

    def compute(t, slot, t_next, slot_next):
        phase = [0]

        def issue_some():
            if t_next is not None:
                issue(t_next, slot_next, phase[0] * per_phase, (phase[0] + 1) * per_phase)
            phase[0] += 1

        wait(slot)
        gt = g_ref[pl.ds(t, 1), :]
        xt = x_ref[pl.ds(t, 1), :]
        acc = chunk(slot, 0) * xt[:, :LANE]
        issue_some()
        for c in range(1, n_chunk):
            acc = acc + chunk(slot, c) * xt[:, c * LANE:(c + 1) * LANE]
            issue_some()
        act = jnp.sum(acc, axis=1, keepdims=True)
        g_col = jnp.sum(jnp.where(eye, gt, 0.0), axis=1, keepdims=True)
        w = g_col * (0.5 * act * (1.0 + lax.erf(act * np.float32(np.sqrt(0.5)))))
        outs = []
        for c in range(n_chunk):
            outs.append(jnp.sum(w * chunk(slot, n_chunk + c), axis=0, keepdims=True))
            issue_some()
        o_ref[pl.ds(t, 1), :] = jnp.concatenate(outs, axis=1)

    ahead = PEER_NBUF - 1
    for s in range(ahead):
        issue(s, s, 0, PEER_PAIRS)

    def group(base, n_issue):
        for j in range(PEER_NBUF):
            if j < n_issue:
                compute(base + j, j, base + j + ahead, (j + ahead) % PEER_NBUF)
            else:
                compute(base + j, j, None, None)

    n_group = n_tok // PEER_NBUF

    def steady(gi, carry):
        group(gi * PEER_NBUF, PEER_NBUF)
        return carry

    lax.fori_loop(0, n_group - 1, steady, 0)
    group((n_group - 1) * PEER_NBUF, PEER_NBUF - ahead)


def _peer_gather(x, idx, g, uv_tab):
    n, d = x.shape
    return pl.pallas_call(
        _peer_gather_kernel,
        grid=(n // PEER_TB,),
        in_specs=[
            pl.BlockSpec((PEER_TB, PEER_PAIRS), lambda i: (i, 0), memory_space=pltpu.SMEM),
            pl.BlockSpec((PEER_TB, d), lambda i: (i, 0)),
            pl.BlockSpec((PEER_TB, PEER_PAIRS), lambda i: (i, 0)),
            pl.BlockSpec(memory_space=pl.ANY),
        ],
        out_specs=pl.BlockSpec((PEER_TB, d), lambda i: (i, 0)),
        out_shape=jax.ShapeDtypeStruct((n, d), jnp.float32),
        scratch_shapes=[pltpu.VMEM((PEER_NBUF, PEER_PAIRS * SLAB_PITCH, LANE), jnp.float32), pltpu.SemaphoreType.DMA((PEER_NBUF,))],
        name="peer_gather",
    )(idx * SLAB_ROWS, x, g, uv_tab)


def _top_rows(s, k, payload=None):
    n_rows = s.shape[0]
    row = lax.broadcasted_iota(jnp.int32, s.shape, 0)
    out_row = lax.broadcasted_iota(jnp.int32, (k, s.shape[1]), 0)
    vals = jnp.zeros((k, s.shape[1]), jnp.float32)
    sel = jnp.zeros((k, s.shape[1]), jnp.int32)
    for j in range(k):
        m = jnp.max(s, axis=0, keepdims=True)
        r = jnp.min(jnp.where(s == m, row, n_rows), axis=0, keepdims=True)
        hit = row == r
        s = jnp.where(hit, -jnp.inf, s)
        what = r if payload is None else jnp.max(jnp.where(hit, payload, -1), axis=0, keepdims=True)
        vals = jnp.where(out_row == j, m, vals)
        sel = jnp.where(out_row == j, what, sel)
    return vals, sel


def _peer_select_kernel(x_ref, sc_ref, sh_ref, wq_ref, keys_ref, xm_ref, idx_ref, g_ref, q_scr, sv_scr, si_scr, bi_scr, bg_scr):
    f32 = jnp.float32
    n_hc = 2 * H_P
    h = x_ref[...] * (1.0 + sc_ref[0]) + sh_ref[0]
    xm_ref[...] = h
    q = jnp.dot(h.astype(jnp.bfloat16), wq_ref[...], preferred_element_type=f32)
    for hc in range(n_hc):
        q_scr[hc] = q[:, hc * D_HALF:(hc + 1) * D_HALF].astype(jnp.bfloat16)

    def sub_key_top(hc, carry):
        s = lax.dot_general(keys_ref[hc], q_scr[hc], (((1,), (1,)), ((), ())), preferred_element_type=f32)
        sv_scr[hc], si_scr[hc] = _top_rows(s, TOPK)
        return carry

    lax.fori_loop(0, n_hc, sub_key_top, 0)

    def pair_top(hd, carry):
        sv0, sv1 = sv_scr[2 * hd], sv_scr[2 * hd + 1]
        si0, si1 = si_scr[2 * hd], si_scr[2 * hd + 1]
        half = TOPK // 2
        cand = [sv0[0:1] + sv1]
        cidx = [si0[0:1] * N_KEYS + si1]
        for a in range(1, half):
            cand.append(sv0[a:a + 1] + sv1[:half])
            cidx.append(si0[a:a + 1] * N_KEYS + si1[:half])
        cand.append(sv0[half:] + sv1[0:1])
        cidx.append(si0[half:] * N_KEYS + si1[0:1])
        best, eid = _top_rows(jnp.concatenate(cand, axis=0), TOPK, payload=jnp.concatenate(cidx, axis=0))
        e = jnp.exp(best - best[0:1])
        bi_scr[hd] = eid
        bg_scr[hd] = e / jnp.sum(e, axis=0, keepdims=True)
        return carry

    lax.fori_loop(0, H_P, pair_top, 0)
    tm = x_ref.shape[0]
    idx_ref[...] = bi_scr[...].reshape(PEER_PAIRS, tm).T
    g_ref[...] = bg_scr[...].reshape(PEER_PAIRS, tm).T


def _peer_select(x, sc, sh, w_q, sub_keys):
    n_seq, seq_len, d = x.shape
    n = n_seq * seq_len
    n_hc = 2 * H_P
    per_seq = seq_len // SEL_TM
    tok = lambda i: (i, 0)
    seq = lambda i: (i // per_seq, 0, 0)
    return pl.pallas_call(
        _peer_select_kernel,
        grid=(n // SEL_TM,),
        in_specs=[
            pl.BlockSpec((SEL_TM, d), tok),
            pl.BlockSpec((1, 1, d), seq),
            pl.BlockSpec((1, 1, d), seq),
            pl.BlockSpec((d, H_P * D_QK), lambda i: (0, 0)),
            pl.BlockSpec((n_hc, N_KEYS, D_HALF), lambda i: (0, 0, 0)),
        ],
        out_specs=[pl.BlockSpec((SEL_TM, d), tok), pl.BlockSpec((SEL_TM, PEER_PAIRS), tok), pl.BlockSpec((SEL_TM, PEER_PAIRS), tok)],
        out_shape=[jax.ShapeDtypeStruct((n, d), jnp.float32), jax.ShapeDtypeStruct((n, PEER_PAIRS), jnp.int32), jax.ShapeDtypeStruct((n, PEER_PAIRS), jnp.float32)],
        scratch_shapes=[
            pltpu.VMEM((n_hc, SEL_TM, D_HALF), jnp.bfloat16),
            pltpu.VMEM((n_hc, TOPK, SEL_TM), jnp.float32),
            pltpu.VMEM((n_hc, TOPK, SEL_TM), jnp.int32),
            pltpu.VMEM((H_P, TOPK, SEL_TM), jnp.int32),
            pltpu.VMEM((H_P, TOPK, SEL_TM), jnp.float32),
        ],
        name="peer_select",
    )(x.reshape(n, d), sc, sh, w_q.astype(jnp.bfloat16), sub_keys.reshape(n_hc, N_KEYS, D_HALF).astype(jnp.bfloat16))


def _residual_norm_kernel(x_ref, f_ref, g_ref, lng_ref, lnb_ref, o_ref):
    o_ref[...] = _layer_norm_rows(ALPHA * x_ref[...] + g_ref[0] * f_ref[...], lng_ref[...], lnb_ref[...])


def _residual_norm(x, f, gate, ln_g, ln_b):
    n_seq, seq_len, d = x.shape
    n = n_seq * seq_len
    per_seq = seq_len // ROW_TILE
    row = pl.BlockSpec((ROW_TILE, d), lambda i: (i, 0))
    vec = pl.BlockSpec((1, d), lambda i: (0, 0))
    return pl.pallas_call(
        _residual_norm_kernel,
        grid=(n // ROW_TILE,),
        in_specs=[row, row, pl.BlockSpec((1, 1, d), lambda i: (i // per_seq, 0, 0)), vec, vec],
        out_specs=row,
        out_shape=jax.ShapeDtypeStruct((n, d), jnp.float32),
        name="residual_norm",
    )(x.reshape(n, d), f.reshape(n, d), gate, ln_g.reshape(1, d), ln_b.reshape(1, d)).reshape(x.shape)


def _peer(x, sc, sh, w_q, sub_keys, uv_tab):
    xm, idx, g = _peer_select(x, sc, sh, w_q, sub_keys)
    return _peer_gather(xm, idx, g, uv_tab).reshape(x.shape)


def kernel(x_prompt, x_sample, c, c_ctx, state_mlstm_C, state_mlstm_n, state_mlstm_m, state_ret, state_ssd, w_mod, b_mod, ln1_g, ln1_b, ln2_g, ln2_b, even_w_in, mlstm_i_bias, mlstm_f_bias, ret_log_decay, even_gn_a, even_gn_b, even_w_out, odd_w_in, odd_conv_w, odd_conv_b, odd_dt_bias, odd_a_log, odd_d, odd_norm_w, odd_w_out, peer_w_q, peer_sub_keys, peer_u, peer_v):
    f32 = jnp.float32
    bp = x_prompt.shape[0]
    xp = x_prompt
    xs = x_sample + _grid_pos_embed(x_sample.shape[1], x_sample.dtype)[None]
    new_C, new_n, new_m, new_S, new_h = [], [], [], [], []
    bcast = lambda m: jnp.broadcast_to(m, (bp, 1, D_MODEL))
    for l in range(DEPTH):
        sh1p, sc1p, g1p, sh2p, sc2p, g2p = [bcast(m) for m in _modulation(c_ctx[None, :], w_mod[l], b_mod[l])]
        sh1s, sc1s, g1s, sh2s, sc2s, g2s = _modulation(c, w_mod[l], b_mod[l])
        j = l // 2
        if l % 2 == 0:
            wts = _even_weights(even_w_in[j], mlstm_i_bias[j], mlstm_f_bias[j], ret_log_decay[j], even_gn_a[j], even_gn_b[j], even_w_out[j])
            zero = (jnp.zeros((bp, N_DIR, H_A, DK_A, DV_A), f32), jnp.zeros((bp, N_DIR, H_A, DK_A), f32), jnp.zeros((bp, N_DIR, H_A), f32), jnp.zeros((bp, N_DIR, H_B, DK_B, DV_B), f32))
            xp, fin = _even_layer(xp, sc1p, sh1p, g1p, wts, ln1_g[l], ln1_b[l], *zero)
            xs, _ = _even_layer(xs, sc1s, sh1s, g1s, wts, ln1_g[l], ln1_b[l], state_mlstm_C[:, j], state_mlstm_n[:, j], state_mlstm_m[:, j], state_ret[:, j])
            new_C.append(fin[0])
            new_n.append(fin[1])
            new_m.append(fin[2])
            new_S.append(fin[3])
        else:
            wts = _odd_weights(odd_w_in[j], odd_dt_bias[j], odd_a_log[j], odd_d[j], odd_norm_w[j], odd_w_out[j])
            xp, fin = _odd_layer(xp, sc1p, sh1p, g1p, wts, odd_conv_w[j], odd_conv_b[j], ln1_g[l], ln1_b[l], jnp.zeros((bp, N_DIR, H_C, P_C, N_C), f32))
            xs, _ = _odd_layer(xs, sc1s, sh1s, g1s, wts, odd_conv_w[j], odd_conv_b[j], ln1_g[l], ln1_b[l], state_ssd[:, j])
            new_h.append(fin)
        uv_tab = jnp.concatenate([peer_u[l], peer_v[l]], axis=1).reshape(-1, LANE)
        fp = _peer(xp, sc2p, sh2p, peer_w_q[l], peer_sub_keys[l], uv_tab)
        fs = _peer(xs, sc2s, sh2s, peer_w_q[l], peer_sub_keys[l], uv_tab)
        xp = _residual_norm(xp, fp, g2p, ln2_g[l], ln2_b[l])
        xs = _residual_norm(xs, fs, g2s, ln2_g[l], ln2_b[l])
    return (xp, xs, jnp.stack(new_C, axis=1), jnp.stack(new_n, axis=1), jnp.stack(new_m, axis=1), jnp.stack(new_S, axis=1), jnp.stack(new_h, axis=1))
```

```python
import functools

import jax
import jax.numpy as jnp
import numpy as np
from jax import lax
from jax.experimental import pallas as pl
from jax.experimental.pallas import tpu as pltpu

D_MODEL = 1024
DEPTH = 4
GRID_W = 64
CHUNK = 64
N_DIR = 2
ALPHA = (2.0 * DEPTH) ** 0.25
H_A = 4
DK_A = 128
DV_A = 128
H_B = 4
DK_B = 128
DV_B = 128
W_A = H_A * DV_A
W_B = H_B * DV_B
EVEN_SIZES = (H_A * DK_A, H_A * DK_A, W_A, W_A, N_DIR * H_A, N_DIR * H_A, H_B * DK_B, H_B * DK_B, W_B, W_B)
EVEN_SPLITS = [int(s) for s in np.cumsum(EVEN_SIZES)[:-1]]
D_INNER = 2 * D_MODEL
P_C = 64
H_C = D_INNER // P_C
G_C = 4
J_C = H_C // G_C
N_C = 128
D_CONV = 5
CONV_DIM = D_INNER + 2 * G_C * N_C
N_KEYS = 128
H_P = 8
D_QK = 256
D_HALF = D_QK // 2
TOPK = 16

LANE = 128
ROW_TILE = 256
PEER_PAIRS = H_P * TOPK
SEL_TM = 256
PEER_TB = 256
PEER_NBUF = 8
SLAB_ROWS = 2 * D_MODEL // LANE
SLAB_PITCH = SLAB_ROWS + 4


def _mod_proj_kernel(x_ref, sc_ref, sh_ref, w_ref, o_ref):
    h = x_ref[...] * (1.0 + sc_ref[0]) + sh_ref[0]
    o_ref[...] = jnp.dot(h.astype(jnp.bfloat16), w_ref[...], preferred_element_type=jnp.float32)


def _mod_proj(x, sc, sh, wb):
    n_seq, seq_len, d = x.shape
    n = n_seq * seq_len
    n_out = wb.shape[1]
    per_seq = seq_len // ROW_TILE
    seq = lambda i: (i // per_seq, 0, 0)
    return pl.pallas_call(
        _mod_proj_kernel,
        grid=(n // ROW_TILE,),
        in_specs=[pl.BlockSpec((ROW_TILE, d), lambda i: (i, 0)), pl.BlockSpec((1, 1, d), seq), pl.BlockSpec((1, 1, d), seq), pl.BlockSpec((d, n_out), lambda i: (0, 0))],
        out_specs=pl.BlockSpec((ROW_TILE, n_out), lambda i: (i, 0)),
        out_shape=jax.ShapeDtypeStruct((n, n_out), jnp.float32),
        name="mod_proj",
    )(x.reshape(n, d), sc, sh, wb)


def _layer_norm_rows(y, g, b, eps=1e-5):
    mu = jnp.mean(y, axis=-1, keepdims=True)
    yc = y - mu
    var = jnp.mean(yc * yc, axis=-1, keepdims=True)
    return yc * lax.rsqrt(var + eps) * g + b


EVEN_MAIN = 8 * W_A
EVEN_COLS = EVEN_MAIN + LANE
N_HD = N_DIR * H_A


def _even_scan_kernel(pf_ref, pb_ref, c0_ref, n0_ref, m0_ref, s0_ref, gb_ref, lg_ref,
                      hf_ref, hb_ref, cf_ref, nf_ref, mf_ref, sf_ref, c_scr, n_scr, m_scr, s_scr):
    f32, bf16 = jnp.float32, jnp.bfloat16
    ci = pl.program_id(1)

    @pl.when(ci == 0)
    def _():
        c_scr[...] = c0_ref[0]
        n_scr[...] = n0_ref[0]
        m_scr[...] = m0_ref[0]
        s_scr[...] = s0_ref[0]

    t_i = lax.broadcasted_iota(jnp.int32, (CHUNK, CHUNK), 0)
    s_i = lax.broadcasted_iota(jnp.int32, (CHUNK, CHUNK), 1)
    t_col = lax.broadcasted_iota(jnp.int32, (CHUNK, 1), 0).astype(f32)
    nt = (((1,), (1,)), ((), ()))
    tn = (((0,), (0,)), ((), ()))
    hi = lax.Precision.HIGHEST

    for d, (p_ref, h_ref) in enumerate(((pf_ref, hf_ref), (pb_ref, hb_ref))):
        mask = (s_i <= t_i) if d == 0 else (s_i >= t_i)
        tri = mask.astype(f32)
        last = CHUNK - 1 if d == 0 else 0
        gates = p_ref[:, EVEN_MAIN:] + gb_ref[...]
        lf = jax.nn.log_sigmoid(gates)
        b_cols = jnp.dot(tri, lf, precision=hi, preferred_element_type=f32)
        b_rows = lax.dot_general(lf.T, tri, nt, precision=hi, preferred_element_type=f32)
        li_rows = gates.T
        diff = (t_i - s_i if d == 0 else s_i - t_i).astype(f32)
        tau = t_col if d == 0 else (CHUNK - 1.0) - t_col
        base = 4 * W_A
        qs, ks, vs, qk, qst = [], [], [], [], []
        for hd in range(H_A):
            j = d * H_A + hd
            q = p_ref[:, hd * DK_A:(hd + 1) * DK_A]
            k = p_ref[:, W_A + hd * DK_A:W_A + (hd + 1) * DK_A] * DK_A ** -0.5
            vb_ = p_ref[:, 2 * W_A + hd * DV_A:2 * W_A + (hd + 1) * DV_A].astype(bf16)
            qb_ = q.astype(bf16)
            qs.append(q), ks.append(k), vs.append(vb_)
            qk.append(lax.dot_general(qb_, k.astype(bf16), nt, preferred_element_type=f32))
            qst.append(jnp.dot(qb_, c_scr[j].astype(bf16), preferred_element_type=f32))
        for hd in range(H_B):
            j = d * H_B + hd
            q = p_ref[:, base + hd * DK_B:base + (hd + 1) * DK_B] * DK_B ** -0.5
            k = p_ref[:, base + W_B + hd * DK_B:base + W_B + (hd + 1) * DK_B]
            vb_ = p_ref[:, base + 2 * W_B + hd * DV_B:base + 2 * W_B + (hd + 1) * DV_B].astype(bf16)
            qb_ = q.astype(bf16)
            qs.append(q), ks.append(k), vs.append(vb_)
            qk.append(lax.dot_general(qb_, k.astype(bf16), nt, preferred_element_type=f32))
            qst.append(jnp.dot(qb_, s_scr[j].astype(bf16), preferred_element_type=f32))
        s_all, kw, post = [], [], []
        for hd in range(H_A):
            j = d * H_A + hd
            b_col = b_cols[:, N_HD + j:N_HD + j + 1]
            b_row = b_rows[N_HD + j:N_HD + j + 1, :]
            li_col = gates[:, j:j + 1]
            li_row = li_rows[j:j + 1, :]
            m_prev = m_scr[j][:, 0:1]
            dlog = jnp.where(mask, b_col - b_row + li_row, -jnp.inf)
            inter = b_col + m_prev
            m_t = jnp.maximum(inter, jnp.max(dlog, axis=1, keepdims=True))
            s = qk[hd] * jnp.exp(dlog - m_t)
            w_int = jnp.exp(inter - m_t)
            den = jnp.sum(s, axis=1, keepdims=True) + w_int * jnp.sum(qs[hd] * n_scr[j], axis=1, keepdims=True)
            b_last = b_col[last:last + 1]
            g_col = b_last - b_col + li_col
            m_new = jnp.maximum(b_last + m_prev, jnp.max(g_col, axis=0, keepdims=True))
            wg = jnp.exp(g_col - m_new)
            dec = jnp.exp(b_last + m_prev - m_new)
            s_all.append(s.astype(bf16))
            kw.append(ks[hd] * wg)
            post.append((w_int, jnp.maximum(jnp.abs(den), jnp.exp(-m_t)), dec, m_new))
        for hd in range(H_B):
            lg = lg_ref[d, hd]
            decay = jnp.where(diff >= 0, jnp.exp(jnp.maximum(diff, 0.0) * lg), 0.0)
            xi = jnp.exp((tau + 1.0) * lg)
            zeta = jnp.exp((CHUNK - 1.0 - tau) * lg)
            g_end = jnp.exp(jnp.full((1, 1), CHUNK, f32) * lg)
            s_all.append((qk[H_A + hd] * decay).astype(bf16))
            kw.append(ks[H_A + hd] * zeta)
            post.append((xi, g_end))
        for hd in range(H_A):
            j = d * H_A + hd
            w_int, den, dec, m_new = post[hd]
            num = jnp.dot(s_all[hd], vs[hd], preferred_element_type=f32) + w_int * qst[hd]
            h_ref[:, hd * DV_A:(hd + 1) * DV_A] = num / den
            c_scr[j] = dec * c_scr[j] + lax.dot_general(kw[hd].astype(bf16), vs[hd], tn, preferred_element_type=f32)
            n_scr[j] = dec * n_scr[j] + jnp.sum(kw[hd], axis=0, keepdims=True)
            m_scr[j] = jnp.broadcast_to(m_new, (1, DK_A))
        for hd in range(H_B):
            j = d * H_B + hd
            xi, g_end = post[H_A + hd]
            o = jnp.dot(s_all[H_A + hd], vs[H_A + hd], preferred_element_type=f32) + xi * qst[H_A + hd]
            h_ref[:, W_A + hd * DV_B:W_A + (hd + 1) * DV_B] = o
            s_scr[j] = g_end * s_scr[j] + lax.dot_general(kw[H_A + hd].astype(bf16), vs[H_A + hd], tn, preferred_element_type=f32)

    @pl.when(ci == pl.num_programs(1) - 1)
    def _():
        cf_ref[0] = c_scr[...]
        nf_ref[0] = n_scr[...]
        mf_ref[0] = m_scr[...]
        sf_ref[0] = s_scr[...]


def _even_scan(proj, n_seq, seq_len, c0, n0, m0, s0, gate_bias, log_gamma):
    nc = seq_len // CHUNK
    f32 = jnp.float32
    st4 = lambda s, c: (s, 0, 0, 0)
    big = pl.BlockSpec((1, N_HD, DK_A, DV_A), st4)
    small = pl.BlockSpec((1, N_HD, 1, DK_A), st4)
    fwd = lambda s, c: (s * nc + c, 0)
    bwd = lambda s, c: (s * nc + (nc - 1 - c), 0)
    n = n_seq * seq_len
    return pl.pallas_call(
        _even_scan_kernel,
        grid=(n_seq, nc),
        in_specs=[pl.BlockSpec((CHUNK, EVEN_COLS), fwd), pl.BlockSpec((CHUNK, EVEN_COLS), bwd), big, small, small, big,
                  pl.BlockSpec((1, LANE), lambda s, c: (0, 0)), pl.BlockSpec(memory_space=pltpu.SMEM)],
        out_specs=[pl.BlockSpec((CHUNK, W_A + W_B), fwd), pl.BlockSpec((CHUNK, W_A + W_B), bwd), big, small, small, big],
        out_shape=[jax.ShapeDtypeStruct((n, W_A + W_B), f32), jax.ShapeDtypeStruct((n, W_A + W_B), f32),
                   jax.ShapeDtypeStruct(c0.shape, f32), jax.ShapeDtypeStruct(n0.shape, f32), jax.ShapeDtypeStruct(m0.shape, f32), jax.ShapeDtypeStruct(s0.shape, f32)],
        scratch_shapes=[pltpu.VMEM((N_HD, DK_A, DV_A), f32), pltpu.VMEM((N_HD, 1, DK_A), f32), pltpu.VMEM((N_HD, 1, DK_A), f32), pltpu.VMEM((N_HD, DK_B, DV_B), f32)],
        name="even_scan",
    )(proj, proj, c0, n0, m0, s0, gate_bias, log_gamma)


def _even_post_kernel(hf_ref, hb_ref, oa_ref, gb_ref, x_ref, g1_ref, gn_ref, w_ref, lng_ref, lnb_ref, o_ref):
    hs = hf_ref[...] + hb_ref[...]
    parts = []
    for hd in range(H_A + H_B):
        hh = hs[:, hd * DV_A:(hd + 1) * DV_A]
        mu = jnp.mean(hh, axis=-1, keepdims=True)
        hc = hh - mu
        var = jnp.mean(hc * hc, axis=-1, keepdims=True)
        parts.append(hc * lax.rsqrt(var + 1e-5))
    normed = jnp.concatenate(parts, axis=1) * gn_ref[...]
    gb = gb_ref[...]
    act = jnp.concatenate([jax.nn.sigmoid(oa_ref[...]), gb * jax.nn.sigmoid(gb)], axis=1)
    ms = jnp.dot((normed * act).astype(jnp.bfloat16), w_ref[...], preferred_element_type=jnp.float32)
    o_ref[...] = _layer_norm_rows(ALPHA * x_ref[...] + g1_ref[0] * ms, lng_ref[...], lnb_ref[...])


def _even_post(hf, hb, proj, x, g1, gn, w_out_b, ln_g, ln_b):
    n_seq, seq_len, d = x.shape
    n = n_seq * seq_len
    per_seq = seq_len // ROW_TILE
    row = lambda i: (i, 0)
    full = lambda i: (0, 0)
    return pl.pallas_call(
        _even_post_kernel,
        grid=(n // ROW_TILE,),
        in_specs=[pl.BlockSpec((ROW_TILE, W_A + W_B), row), pl.BlockSpec((ROW_TILE, W_A + W_B), row),
                  pl.BlockSpec((ROW_TILE, W_A), lambda i: (i, 3)), pl.BlockSpec((ROW_TILE, W_B), lambda i: (i, 7)),
                  pl.BlockSpec((ROW_TILE, d), row), pl.BlockSpec((1, 1, d), lambda i: (i // per_seq, 0, 0)),
                  pl.BlockSpec((1, W_A + W_B), full), pl.BlockSpec((W_A + W_B, d), full), pl.BlockSpec((1, d), full), pl.BlockSpec((1, d), full)],
        out_specs=pl.BlockSpec((ROW_TILE, d), row),
        out_shape=jax.ShapeDtypeStruct((n, d), jnp.float32),
        name="even_post",
    )(hf, hb, proj, proj, x.reshape(n, d), g1, gn, w_out_b, ln_g, ln_b).reshape(x.shape)


def _even_weights(w_in, i_bias, f_bias, log_decay, gn_a, gn_b, w_out):
    qa, ka, va, oa, ia, fa, qb, kb, vb, gb = jnp.split(w_in, EVEN_SPLITS, axis=1)
    gates = jnp.pad(jnp.concatenate([ia, fa], axis=1), ((0, 0), (0, LANE - 2 * N_HD)))
    w_in_b = jnp.concatenate([qa, ka, va, oa, qb, kb, vb, gb, gates], axis=1).astype(jnp.bfloat16)
    gate_bias = jnp.pad(jnp.concatenate([i_bias.reshape(-1), f_bias.reshape(-1)]), (0, LANE - 2 * N_HD)).reshape(1, LANE)
    log_gamma = -jnp.exp(log_decay.astype(jnp.float32))
    gn = jnp.concatenate([gn_a.reshape(-1), gn_b.reshape(-1)]).reshape(1, W_A + W_B)
    return w_in_b, gate_bias, log_gamma, gn, w_out.astype(jnp.bfloat16)


def _even_layer(x, sc1, sh1, g1, wts, ln_g, ln_b, c0, n0, m0, s0):
    w_in_b, gate_bias, log_gamma, gn, w_out_b = wts
    n_seq, seq_len, d = x.shape
    proj = _mod_proj(x, sc1, sh1, w_in_b)
    st = lambda a: a.reshape(n_seq, N_HD, DK_A, DV_A)
    vec = lambda a: a.reshape(n_seq, N_HD, 1, DK_A)
    m0b = jnp.broadcast_to(m0.reshape(n_seq, N_HD, 1, 1), (n_seq, N_HD, 1, DK_A))
    hf, hb, cf, nf, mf, sf = _even_scan(proj, n_seq, seq_len, st(c0), vec(n0), m0b, st(s0), gate_bias, log_gamma)
    x_new = _even_post(hf, hb, proj, x, g1, gn, w_out_b, ln_g.reshape(1, d), ln_b.reshape(1, d))
    fin = (cf.reshape(n_seq, N_DIR, H_A, DK_A, DV_A), nf.reshape(n_seq, N_DIR, H_A, DK_A), mf[:, :, 0, 0].reshape(n_seq, N_DIR, H_A), sf.reshape(n_seq, N_DIR, H_B, DK_B, DV_B))
    return x_new, fin


ODD_DT = D_INNER + CONV_DIM
ODD_COLS = ODD_DT + LANE
CONV_TILE = 1024
HALO = 8
GW = J_C * P_C


def _conv_kernel(x_ref, prev_ref, next_ref, w_ref, b_ref, o_ref, *, per_seq):
    i = pl.program_id(0)
    pad = D_CONV // 2
    rows = x_ref.shape[0]
    first = lax.rem(i, per_seq) == 0
    last = lax.rem(i, per_seq) == per_seq - 1
    prev = jnp.where(first, 0.0, prev_ref[HALO - pad:, :])
    nxt = jnp.where(last, 0.0, next_ref[:pad, :])
    ext = jnp.concatenate([prev, x_ref[...], nxt], axis=0)
    acc = b_ref[...] + ext[0:rows] * w_ref[0:1, :]
    for k in range(1, D_CONV):
        acc = acc + ext[k:k + rows] * w_ref[k:k + 1, :]
    o_ref[...] = acc * jax.nn.sigmoid(acc)


def _conv_silu(proj, seq_len, conv_w, conv_b):
    n = proj.shape[0]
    per_seq = seq_len // ROW_TILE
    c0 = D_INNER // CONV_TILE
    hb = ROW_TILE // HALO
    n_hb = n // HALO
    return pl.pallas_call(
        functools.partial(_conv_kernel, per_seq=per_seq),
        grid=(n // ROW_TILE, CONV_DIM // CONV_TILE),
        in_specs=[pl.BlockSpec((ROW_TILE, CONV_TILE), lambda i, j: (i, c0 + j)),
                  pl.BlockSpec((HALO, CONV_TILE), lambda i, j: (jnp.maximum(i * hb - 1, 0), c0 + j)),
                  pl.BlockSpec((HALO, CONV_TILE), lambda i, j: (jnp.minimum((i + 1) * hb, n_hb - 1), c0 + j)),
                  pl.BlockSpec((D_CONV, CONV_TILE), lambda i, j: (0, j)), pl.BlockSpec((1, CONV_TILE), lambda i, j: (0, j))],
        out_specs=pl.BlockSpec((ROW_TILE, CONV_TILE), lambda i, j: (i, j)),
        out_shape=jax.ShapeDtypeStruct((n, CONV_DIM), jnp.float32),
        name="conv_silu",
    )(proj, proj, proj, conv_w, conv_b.reshape(1, CONV_DIM))


def _spread_heads(mat, col0):
    lane = lax.broadcasted_iota(jnp.int32, (mat.shape[0], LANE), 1)
    per_tile = LANE // P_C
    tiles = []
    for j in range(0, J_C, per_tile):
        t = mat[:, col0 + j:col0 + j + 1]
        for r in range(1, per_tile):
            t = jnp.where(lane < r * P_C, t, mat[:, col0 + j + r:col0 + j + r + 1])
        tiles.append(jnp.broadcast_to(t, (mat.shape[0], LANE)))
    return jnp.concatenate(tiles, axis=1)


def _ssd_scan_kernel(af_ref, ab_ref, dtf_ref, dtb_ref, bias_ref, alog_ref, h0_ref, yf_ref, yb_ref, hf_ref, h_scr):
    f32, bf16 = jnp.float32, jnp.bfloat16
    ci = pl.program_id(1)

    @pl.when(ci == 0)
    def _():
        h_scr[...] = h0_ref[0]

    t_i = lax.broadcasted_iota(jnp.int32, (CHUNK, CHUNK), 0)
    s_i = lax.broadcasted_iota(jnp.int32, (CHUNK, CHUNK), 1)
    nt = (((1,), (1,)), ((), ()))
    tn = (((0,), (0,)), ((), ()))
    hi = lax.Precision.HIGHEST
    a_neg = -jnp.exp(alog_ref[...])

    for d, (x_ref, dt_ref, y_ref) in enumerate(((af_ref, dtf_ref, yf_ref), (ab_ref, dtb_ref, yb_ref))):
        mask = (s_i <= t_i) if d == 0 else (s_i >= t_i)
        tri = mask.astype(f32)
        last = CHUNK - 1 if d == 0 else 0
        dtv = jax.nn.softplus(dt_ref[...] + bias_ref[...])
        a = dtv * a_neg
        ac_cols = jnp.dot(tri, a, precision=hi, preferred_element_type=f32)
        ac_rows = lax.dot_general(a.T, tri, nt, precision=hi, preferred_element_type=f32)
        dt_rows = dtv.T
        e_cols = jnp.exp(ac_cols)
        w_cols = jnp.exp(ac_cols[last:last + 1] - ac_cols) * dtv
        bgs, xgs, cbs, inters = [], [], [], []
        for g in range(G_C):
            bg = x_ref[:, D_INNER + g * N_C:D_INNER + (g + 1) * N_C].astype(bf16)
            cg = x_ref[:, D_INNER + G_C * N_C + g * N_C:D_INNER + G_C * N_C + (g + 1) * N_C].astype(bf16)
            bgs.append(bg)
            xgs.append(x_ref[:, g * GW:(g + 1) * GW])
            cbs.append(lax.dot_general(cg, bg, nt, preferred_element_type=f32))
            inters.append(jnp.dot(cg, h_scr[d, g].astype(bf16), preferred_element_type=f32))
        s_all = []
        for g in range(G_C):
            for j in range(J_C):
                col = d * H_C + g * J_C + j
                seg = jnp.where(mask, ac_cols[:, col:col + 1] - ac_rows[col:col + 1, :], -jnp.inf)
                s_all.append((cbs[g] * jnp.exp(seg) * dt_rows[col:col + 1, :]).astype(bf16))
        for g in range(G_C):
            col0 = d * H_C + g * J_C
            e_g = _spread_heads(e_cols, col0)
            intra = [jnp.dot(s_all[g * J_C + j], xgs[g][:, j * P_C:(j + 1) * P_C].astype(bf16), preferred_element_type=f32) for j in range(J_C)]
            y_ref[:, g * GW:(g + 1) * GW] = jnp.concatenate(intra, axis=1) + e_g * inters[g]
            xw = (xgs[g] * _spread_heads(w_cols, col0)).astype(bf16)
            h_scr[d, g] = e_g[last:last + 1] * h_scr[d, g] + lax.dot_general(bgs[g], xw, tn, preferred_element_type=f32)

    @pl.when(ci == pl.num_programs(1) - 1)
    def _():
        hf_ref[0] = h_scr[...]


def _ssd_chunk_scan(xa, proj, n_seq, seq_len, dt_bias_row, a_log_row, h0):
    nc = seq_len // CHUNK
    n = n_seq * seq_len
    f32 = jnp.float32
    fwd = lambda s, c: (s * nc + c, 0)
    bwd = lambda s, c: (s * nc + (nc - 1 - c), 0)
    dcol = ODD_DT // LANE
    st = pl.BlockSpec((1, N_DIR, G_C, N_C, GW), lambda s, c: (s, 0, 0, 0, 0))
    row = pl.BlockSpec((1, LANE), lambda s, c: (0, 0))
    return pl.pallas_call(
        _ssd_scan_kernel,
        grid=(n_seq, nc),
        in_specs=[pl.BlockSpec((CHUNK, CONV_DIM), fwd), pl.BlockSpec((CHUNK, CONV_DIM), bwd),
                  pl.BlockSpec((CHUNK, LANE), lambda s, c: (s * nc + c, dcol)), pl.BlockSpec((CHUNK, LANE), lambda s, c: (s * nc + (nc - 1 - c), dcol)),
                  row, row, st],
        out_specs=[pl.BlockSpec((CHUNK, D_INNER), fwd), pl.BlockSpec((CHUNK, D_INNER), bwd), st],
        out_shape=[jax.ShapeDtypeStruct((n, D_INNER), f32), jax.ShapeDtypeStruct((n, D_INNER), f32), jax.ShapeDtypeStruct(h0.shape, f32)],
        scratch_shapes=[pltpu.VMEM((N_DIR, G_C, N_C, GW), f32)],
        name="ssd_scan",
    )(xa, xa, proj, proj, dt_bias_row, a_log_row, h0)


def _odd_post_kernel(yf_ref, yb_ref, xs_ref, z_ref, x_ref, g1_ref, dsk_ref, nw_ref, w_ref, lng_ref, lnb_ref, o_ref):
    z = z_ref[...]
    y = (yf_ref[...] + yb_ref[...] + dsk_ref[...] * xs_ref[...]) * (z * jax.nn.sigmoid(z))
    y = y * lax.rsqrt(jnp.mean(y * y, axis=-1, keepdims=True) + 1e-5) * nw_ref[...]
    ms = jnp.dot(y.astype(jnp.bfloat16), w_ref[...], preferred_element_type=jnp.float32)
    o_ref[...] = _layer_norm_rows(ALPHA * x_ref[...] + g1_ref[0] * ms, lng_ref[...], lnb_ref[...])


def _odd_post(yf, yb, xa, proj, x, g1, d_row, norm_w, w_out_b, ln_g, ln_b):
    n_seq, seq_len, d = x.shape
    n = n_seq * seq_len
    per_seq = seq_len // ROW_TILE
    row = lambda i: (i, 0)
    full = lambda i: (0, 0)
    wide = pl.BlockSpec((ROW_TILE, D_INNER), row)
    return pl.pallas_call(
        _odd_post_kernel,
        grid=(n // ROW_TILE,),
        in_specs=[wide, wide, wide, wide, pl.BlockSpec((ROW_TILE, d), row), pl.BlockSpec((1, 1, d), lambda i: (i // per_seq, 0, 0)),
                  pl.BlockSpec((1, D_INNER), full), pl.BlockSpec((1, D_INNER), full), pl.BlockSpec((D_INNER, d), full), pl.BlockSpec((1, d), full), pl.BlockSpec((1, d), full)],
        out_specs=pl.BlockSpec((ROW_TILE, d), row),
        out_shape=jax.ShapeDtypeStruct((n, d), jnp.float32),
        name="odd_post",
    )(yf, yb, xa, proj, x.reshape(n, d), g1, d_row, norm_w, w_out_b, ln_g, ln_b).reshape(x.shape)


def _odd_weights(w_in, dt_bias, a_log, d_skip, norm_w, w_out):
    w_in_b = jnp.pad(w_in, ((0, 0), (0, ODD_COLS - w_in.shape[1]))).astype(jnp.bfloat16)
    lane_row = lambda a: jnp.pad(a.astype(jnp.float32).reshape(-1), (0, LANE - N_DIR * H_C)).reshape(1, LANE)
    d_row = jnp.repeat(d_skip.astype(jnp.float32), P_C).reshape(1, D_INNER)
    return w_in_b, lane_row(dt_bias), lane_row(a_log), d_row, norm_w.reshape(1, D_INNER), w_out.astype(jnp.bfloat16)


def _odd_layer(x, sc1, sh1, g1, wts, conv_w, conv_b, ln_g, ln_b, h0):
    w_in_b, dt_row, alog_row, d_row, nw_row, w_out_b = wts
    n_seq, seq_len, d = x.shape
    proj = _mod_proj(x, sc1, sh1, w_in_b)
    xa = _conv_silu(proj, seq_len, conv_w, conv_b)
    to_scan = lambda h: h.reshape(n_seq, N_DIR, G_C, J_C, P_C, N_C).transpose(0, 1, 2, 5, 3, 4).reshape(n_seq, N_DIR, G_C, N_C, GW)
    yf, yb, hfin = _ssd_chunk_scan(xa, proj, n_seq, seq_len, dt_row, alog_row, to_scan(h0))
    x_new = _odd_post(yf, yb, xa, proj, x, g1, d_row, nw_row, w_out_b, ln_g.reshape(1, d), ln_b.reshape(1, d))
    fin = hfin.reshape(n_seq, N_DIR, G_C, N_C, J_C, P_C).transpose(0, 1, 2, 4, 5, 3).reshape(n_seq, N_DIR, H_C, P_C, N_C)
    return x_new, fin


def _modulation(cvec, w, b):
    m = jax.nn.silu(cvec) @ w + b
    return [t[:, None, :] for t in jnp.split(m, 6, axis=-1)]


def _grid_pos_embed(n_tok, dtype):
    rows = n_tok // GRID_W
    r = jnp.repeat(jnp.arange(rows, dtype=jnp.float32), GRID_W)
    col = jnp.tile(jnp.arange(GRID_W, dtype=jnp.float32), rows)
    quarter = D_MODEL // 4
    freqs = 1.0 / (10000.0 ** (jnp.arange(quarter, dtype=jnp.float32) / quarter))
    er = r[:, None] * freqs
    ec = col[:, None] * freqs
    return jnp.concatenate([jnp.sin(er), jnp.cos(er), jnp.sin(ec), jnp.cos(ec)], axis=-1).astype(dtype)


def _expert_slab_copy(tab_ref, buf_ref, sem_ref, row, pair, slot):
    src = tab_ref.at[pl.ds(pl.multiple_of(row, SLAB_ROWS), SLAB_ROWS), :]
    return pltpu.make_async_copy(src, buf_ref.at[slot, pl.ds(pair * SLAB_PITCH, SLAB_ROWS), :], sem_ref.at[slot])


def _peer_gather_kernel(idx_ref, x_ref, g_ref, tab_ref, o_ref, buf_ref, sem_ref):
    n_tok = x_ref.shape[0]
    n_chunk = x_ref.shape[1] // LANE

    per_phase = PEER_PAIRS // (2 * n_chunk)

    def issue(t, slot, p0, p1):
        for p in range(p0, p1):
            _expert_slab_copy(tab_ref, buf_ref, sem_ref, idx_ref[t, p], p, slot).start(priority=p % 2)

    def wait(slot):
        n_rows = PEER_PAIRS * SLAB_ROWS
        pltpu.make_async_copy(tab_ref.at[pl.ds(0, n_rows), :], buf_ref.at[slot, pl.ds(0, n_rows), :], sem_ref.at[slot]).wait()

    def chunk(slot, c):
        return buf_ref[slot, pl.ds(c, PEER_PAIRS, stride=SLAB_PITCH), :]

    eye = lax.broadcasted_iota(jnp.int32, (PEER_PAIRS, PEER_PAIRS), 0) == lax.broadcasted_iota(jnp.int32, (PEER_PAIRS, PEER_PAIRS), 1)

    def compute(t, slot, t_next, slot_next):
        phase = [0]

        def issue_some():
            if t_next is not None:
                issue(t_next, slot_next, phase[0] * per_phase, (phase[0] + 1) * per_phase)
            phase[0] += 1

        wait(slot)
        gt = g_ref[pl.ds(t, 1), :]
        xt = x_ref[pl.ds(t, 1), :]
        acc = chunk(slot, 0) * xt[:, :LANE]
        issue_some()
        for c in range(1, n_chunk):
            acc = acc + chunk(slot, c) * xt[:, c * LANE:(c + 1) * LANE]
            issue_some()
        act = jnp.sum(acc, axis=1, keepdims=True)
        g_col = jnp.sum(jnp.where(eye, gt, 0.0), axis=1, keepdims=True)
        w = g_col * (0.5 * act * (1.0 + lax.erf(act * np.float32(np.sqrt(0.5)))))
        outs = []
        for c in range(n_chunk):
            outs.append(jnp.sum(w * chunk(slot, n_chunk + c), axis=0, keepdims=True))
            issue_some()
        o_ref[pl.ds(t, 1), :] = jnp.concatenate(outs, axis=1)

    ahead = PEER_NBUF - 1
    for s in range(ahead):
        issue(s, s, 0, PEER_PAIRS)

    def group(base, n_issue):
        for j in range(PEER_NBUF):
            if j < n_issue:
                compute(base + j, j, base + j + ahead, (j + ahead) % PEER_NBUF)
            else:
                compute(base + j, j, None, None)

    n_group = n_tok // PEER_NBUF

    def steady(gi, carry):
        group(gi * PEER_NBUF, PEER_NBUF)
        return carry

    lax.fori_loop(0, n_group - 1, steady, 0)
    group((n_group - 1) * PEER_NBUF, PEER_NBUF - ahead)


def _peer_gather(x, idx, g, uv_tab):
    n, d = x.shape
    return pl.pallas_call(
        _peer_gather_kernel,
        grid=(n // PEER_TB,),
        in_specs=[
            pl.BlockSpec((PEER_TB, PEER_PAIRS), lambda i: (i, 0), memory_space=pltpu.SMEM),
            pl.BlockSpec((PEER_TB, d), lambda i: (i, 0)),
            pl.BlockSpec((PEER_TB, PEER_PAIRS), lambda i: (i, 0)),
            pl.BlockSpec(memory_space=pl.ANY),
        ],
        out_specs=pl.BlockSpec((PEER_TB, d), lambda i: (i, 0)),
        out_shape=jax.ShapeDtypeStruct((n, d), jnp.float32),
        scratch_shapes=[pltpu.VMEM((PEER_NBUF, PEER_PAIRS * SLAB_PITCH, LANE), jnp.float32), pltpu.SemaphoreType.DMA((PEER_NBUF,))],
        name="peer_gather",
    )(idx * SLAB_ROWS, x, g, uv_tab)


def _top_rows(s, k, payload=None):
    (vals, sel), = _top_rows_many([s], k, None if payload is None else [payload])
    return vals, sel


def _top_rows_many(ss, k, payloads=None):
    n_rows, width = ss[0].shape
    row = lax.broadcasted_iota(jnp.int32, (n_rows, width), 0)
    out_row = lax.broadcasted_iota(jnp.int32, (k, width), 0)
    ss = list(ss)
    vals = [jnp.zeros((k, width), jnp.float32) for _ in ss]
    sel = [jnp.zeros((k, width), jnp.int32) for _ in ss]
    for j in range(k):
        for i in range(len(ss)):
            m = jnp.max(ss[i], axis=0, keepdims=True)
            r = jnp.min(jnp.where(ss[i] == m, row, n_rows), axis=0, keepdims=True)
            hit = row == r
            ss[i] = jnp.where(hit, -jnp.inf, ss[i])
            what = r if payloads is None else jnp.max(jnp.where(hit, payloads[i], -1), axis=0, keepdims=True)
            vals[i] = jnp.where(out_row == j, m, vals[i])
            sel[i] = jnp.where(out_row == j, what, sel[i])
    return list(zip(vals, sel))


def _peer_select_kernel(x_ref, sc_ref, sh_ref, wq_ref, keys_ref, xm_ref, idx_ref, g_ref, q_scr, sv_scr, si_scr, bi_scr, bg_scr):
    f32 = jnp.float32
    n_hc = 2 * H_P
    h = x_ref[...] * (1.0 + sc_ref[0]) + sh_ref[0]
    xm_ref[...] = h
    q = jnp.dot(h.astype(jnp.bfloat16), wq_ref[...], preferred_element_type=f32)
    for hc in range(n_hc):
        q_scr[hc] = q[:, hc * D_HALF:(hc + 1) * D_HALF].astype(jnp.bfloat16)

    def sub_key_top(hd, carry):
        hcs = (2 * hd, 2 * hd + 1)
        ss = [lax.dot_general(keys_ref[hc], q_scr[hc], (((1,), (1,)), ((), ())), preferred_element_type=f32) for hc in hcs]
        for hc, (v, i) in zip(hcs, _top_rows_many(ss, TOPK)):
            sv_scr[hc], si_scr[hc] = v, i
        return carry

    lax.fori_loop(0, H_P, sub_key_top, 0)

    def pair_top(hd, carry):
        sv0, sv1 = sv_scr[2 * hd], sv_scr[2 * hd + 1]
        si0, si1 = si_scr[2 * hd], si_scr[2 * hd + 1]
        half = TOPK // 2
        cand = [sv0[0:1] + sv1]
        cidx = [si0[0:1] * N_KEYS + si1]
        for a in range(1, half):
            cand.append(sv0[a:a + 1] + sv1[:half])
            cidx.append(si0[a:a + 1] * N_KEYS + si1[:half])
        cand.append(sv0[half:] + sv1[0:1])
        cidx.append(si0[half:] * N_KEYS + si1[0:1])
        best, eid = _top_rows(jnp.concatenate(cand, axis=0), TOPK, payload=jnp.concatenate(cidx, axis=0))
        e = jnp.exp(best - best[0:1])
        bi_scr[hd] = eid
        bg_scr[hd] = e / jnp.sum(e, axis=0, keepdims=True)
        return carry

    lax.fori_loop(0, H_P, pair_top, 0)
    tm = x_ref.shape[0]
    idx_ref[...] = bi_scr[...].reshape(PEER_PAIRS, tm).T
    g_ref[...] = bg_scr[...].reshape(PEER_PAIRS, tm).T


def _peer_select(x, sc, sh, w_q, sub_keys):
    n_seq, seq_len, d = x.shape
    n = n_seq * seq_len
    n_hc = 2 * H_P
    per_seq = seq_len // SEL_TM
    tok = lambda i: (i, 0)
    seq = lambda i: (i // per_seq, 0, 0)
    return pl.pallas_call(
        _peer_select_kernel,
        grid=(n // SEL_TM,),
        in_specs=[
            pl.BlockSpec((SEL_TM, d), tok),
            pl.BlockSpec((1, 1, d), seq),
            pl.BlockSpec((1, 1, d), seq),
            pl.BlockSpec((d, H_P * D_QK), lambda i: (0, 0)),
            pl.BlockSpec((n_hc, N_KEYS, D_HALF), lambda i: (0, 0, 0)),
        ],
        out_specs=[pl.BlockSpec((SEL_TM, d), tok), pl.BlockSpec((SEL_TM, PEER_PAIRS), tok), pl.BlockSpec((SEL_TM, PEER_PAIRS), tok)],
        out_shape=[jax.ShapeDtypeStruct((n, d), jnp.float32), jax.ShapeDtypeStruct((n, PEER_PAIRS), jnp.int32), jax.ShapeDtypeStruct((n, PEER_PAIRS), jnp.float32)],
        scratch_shapes=[
            pltpu.VMEM((n_hc, SEL_TM, D_HALF), jnp.bfloat16),
            pltpu.VMEM((n_hc, TOPK, SEL_TM), jnp.float32),
            pltpu.VMEM((n_hc, TOPK, SEL_TM), jnp.int32),
            pltpu.VMEM((H_P, TOPK, SEL_TM), jnp.int32),
            pltpu.VMEM((H_P, TOPK, SEL_TM), jnp.float32),
        ],
        name="peer_select",
    )(x.reshape(n, d), sc, sh, w_q.astype(jnp.bfloat16), sub_keys.reshape(n_hc, N_KEYS, D_HALF).astype(jnp.bfloat16))


def _residual_norm_kernel(x_ref, f_ref, g_ref, lng_ref, lnb_ref, o_ref):
    o_ref[...] = _layer_norm_rows(ALPHA * x_ref[...] + g_ref[0] * f_ref[...], lng_ref[...], lnb_ref[...])


def _residual_norm(x, f, gate, ln_g, ln_b):
    n_seq, seq_len, d = x.shape
    n = n_seq * seq_len
    per_seq = seq_len // ROW_TILE
    row = pl.BlockSpec((ROW_TILE, d), lambda i: (i, 0))
    vec = pl.BlockSpec((1, d), lambda i: (0, 0))
    return pl.pallas_call(
        _residual_norm_kernel,
        grid=(n // ROW_TILE,),
        in_specs=[row, row, pl.BlockSpec((1, 1, d), lambda i: (i // per_seq, 0, 0)), vec, vec],
        out_specs=row,
        out_shape=jax.ShapeDtypeStruct((n, d), jnp.float32),
        name="residual_norm",
    )(x.reshape(n, d), f.reshape(n, d), gate, ln_g.reshape(1, d), ln_b.reshape(1, d)).reshape(x.shape)


def _peer(x, sc, sh, w_q, sub_keys, uv_tab):
    xm, idx, g = _peer_select(x, sc, sh, w_q, sub_keys)
    return _peer_gather(xm, idx, g, uv_tab).reshape(x.shape)


def kernel(x_prompt, x_sample, c, c_ctx, state_mlstm_C, state_mlstm_n, state_mlstm_m, state_ret, state_ssd, w_mod, b_mod, ln1_g, ln1_b, ln2_g, ln2_b, even_w_in, mlstm_i_bias, mlstm_f_bias, ret_log_decay, even_gn_a, even_gn_b, even_w_out, odd_w_in, odd_conv_w, odd_conv_b, odd_dt_bias, odd_a_log, odd_d, odd_norm_w, odd_w_out, peer_w_q, peer_sub_keys, peer_u, peer_v):
    f32 = jnp.float32
    bp = x_prompt.shape[0]
    xp = x_prompt
    xs = x_sample + _grid_pos_embed(x_sample.shape[1], x_sample.dtype)[None]
    new_C, new_n, new_m, new_S, new_h = [], [], [], [], []
    bcast = lambda m: jnp.broadcast_to(m, (bp, 1, D_MODEL))
    for l in range(DEPTH):
        sh1p, sc1p, g1p, sh2p, sc2p, g2p = [bcast(m) for m in _modulation(c_ctx[None, :], w_mod[l], b_mod[l])]
        sh1s, sc1s, g1s, sh2s, sc2s, g2s = _modulation(c, w_mod[l], b_mod[l])
        j = l // 2
        if l % 2 == 0:
            wts = _even_weights(even_w_in[j], mlstm_i_bias[j], mlstm_f_bias[j], ret_log_decay[j], even_gn_a[j], even_gn_b[j], even_w_out[j])
            zero = (jnp.zeros((bp, N_DIR, H_A, DK_A, DV_A), f32), jnp.zeros((bp, N_DIR, H_A, DK_A), f32), jnp.zeros((bp, N_DIR, H_A), f32), jnp.zeros((bp, N_DIR, H_B, DK_B, DV_B), f32))
            xp, fin = _even_layer(xp, sc1p, sh1p, g1p, wts, ln1_g[l], ln1_b[l], *zero)
            xs, _ = _even_layer(xs, sc1s, sh1s, g1s, wts, ln1_g[l], ln1_b[l], state_mlstm_C[:, j], state_mlstm_n[:, j], state_mlstm_m[:, j], state_ret[:, j])
            new_C.append(fin[0])
            new_n.append(fin[1])
            new_m.append(fin[2])
            new_S.append(fin[3])
        else:
            wts = _odd_weights(odd_w_in[j], odd_dt_bias[j], odd_a_log[j], odd_d[j], odd_norm_w[j], odd_w_out[j])
            xp, fin = _odd_layer(xp, sc1p, sh1p, g1p, wts, odd_conv_w[j], odd_conv_b[j], ln1_g[l], ln1_b[l], jnp.zeros((bp, N_DIR, H_C, P_C, N_C), f32))
            xs, _ = _odd_layer(xs, sc1s, sh1s, g1s, wts, odd_conv_w[j], odd_conv_b[j], ln1_g[l], ln1_b[l], state_ssd[:, j])
            new_h.append(fin)
        uv_tab = jnp.concatenate([peer_u[l], peer_v[l]], axis=1).reshape(-1, LANE)
        fp = _peer(xp, sc2p, sh2p, peer_w_q[l], peer_sub_keys[l], uv_tab)
        fs = _peer(xs, sc2s, sh2s, peer_w_q[l], peer_sub_keys[l], uv_tab)
        xp = _residual_norm(xp, fp, g2p, ln2_g[l], ln2_b[l])
        xs = _residual_norm(xs, fs, g2s, ln2_g[l], ln2_b[l])
    return (xp, xs, jnp.stack(new_C, axis=1), jnp.stack(new_n, axis=1), jnp.stack(new_m, axis=1), jnp.stack(new_S, axis=1), jnp.stack(new_h, axis=1))
```

```python
import functools

import jax
import jax.numpy as jnp
import numpy as np
from jax import lax
from jax.experimental import pallas as pl
from jax.experimental.pallas import tpu as pltpu

D_MODEL = 1024
DEPTH = 4
GRID_W = 64
CHUNK = 64
N_DIR = 2
ALPHA = (2.0 * DEPTH) ** 0.25
H_A = 4
DK_A = 128
DV_A = 128
H_B = 4
DK_B = 128
DV_B = 128
W_A = H_A * DV_A
W_B = H_B * DV_B
EVEN_SIZES = (H_A * DK_A, H_A * DK_A, W_A, W_A, N_DIR * H_A, N_DIR * H_A, H_B * DK_B, H_B * DK_B, W_B, W_B)
EVEN_SPLITS = [int(s) for s in np.cumsum(EVEN_SIZES)[:-1]]
D_INNER = 2 * D_MODEL
P_C = 64
H_C = D_INNER // P_C
G_C = 4
J_C = H_C // G_C
N_C = 128
D_CONV = 5
CONV_DIM = D_INNER + 2 * G_C * N_C
N_KEYS = 128
H_P = 8
D_QK = 256
D_HALF = D_QK // 2
TOPK = 16

LANE = 128
ROW_TILE = 256
PEER_PAIRS = H_P * TOPK
SEL_TM = 256
PEER_TB = 256
PEER_NBUF = 8
SLAB_ROWS = 2 * D_MODEL // LANE
SLAB_PITCH = SLAB_ROWS + 4


def _mod_proj_kernel(x_ref, sc_ref, sh_ref, w_ref, o_ref):
    h = x_ref[...] * (1.0 + sc_ref[0]) + sh_ref[0]
    o_ref[...] = jnp.dot(h.astype(jnp.bfloat16), w_ref[...], preferred_element_type=jnp.float32)


def _mod_proj(x, sc, sh, wb):
    n_seq, seq_len, d = x.shape
    n = n_seq * seq_len
    n_out = wb.shape[1]
    per_seq = seq_len // ROW_TILE
    seq = lambda i: (i // per_seq, 0, 0)
    return pl.pallas_call(
        _mod_proj_kernel,
        grid=(n // ROW_TILE,),
        in_specs=[pl.BlockSpec((ROW_TILE, d), lambda i: (i, 0)), pl.BlockSpec((1, 1, d), seq), pl.BlockSpec((1, 1, d), seq), pl.BlockSpec((d, n_out), lambda i: (0, 0))],
        out_specs=pl.BlockSpec((ROW_TILE, n_out), lambda i: (i, 0)),
        out_shape=jax.ShapeDtypeStruct((n, n_out), jnp.float32),
        name="mod_proj",
    )(x.reshape(n, d), sc, sh, wb)


def _layer_norm_rows(y, g, b, eps=1e-5):
    mu = jnp.mean(y, axis=-1, keepdims=True)
    yc = y - mu
    var = jnp.mean(yc * yc, axis=-1, keepdims=True)
    return yc * lax.rsqrt(var + eps) * g + b


EVEN_MAIN = 8 * W_A
EVEN_COLS = EVEN_MAIN + LANE
N_HD = N_DIR * H_A


def _even_scan_kernel(pf_ref, pb_ref, c0_ref, n0_ref, m0_ref, s0_ref, gb_ref, lg_ref,
                      hf_ref, hb_ref, cf_ref, nf_ref, mf_ref, sf_ref, c_scr, n_scr, m_scr, s_scr):
    f32, bf16 = jnp.float32, jnp.bfloat16
    ci = pl.program_id(1)

    @pl.when(ci == 0)
    def _():
        c_scr[...] = c0_ref[0]
        n_scr[...] = n0_ref[0]
        m_scr[...] = m0_ref[0]
        s_scr[...] = s0_ref[0]

    t_i = lax.broadcasted_iota(jnp.int32, (CHUNK, CHUNK), 0)
    s_i = lax.broadcasted_iota(jnp.int32, (CHUNK, CHUNK), 1)
    t_col = lax.broadcasted_iota(jnp.int32, (CHUNK, 1), 0).astype(f32)
    nt = (((1,), (1,)), ((), ()))
    tn = (((0,), (0,)), ((), ()))
    hi = lax.Precision.HIGHEST

    for d, (p_ref, h_ref) in enumerate(((pf_ref, hf_ref), (pb_ref, hb_ref))):
        mask = (s_i <= t_i) if d == 0 else (s_i >= t_i)
        tri = mask.astype(f32)
        last = CHUNK - 1 if d == 0 else 0
        gates = p_ref[:, EVEN_MAIN:] + gb_ref[...]
        lf = jax.nn.log_sigmoid(gates)
        b_cols = jnp.dot(tri, lf, precision=hi, preferred_element_type=f32)
        b_rows = lax.dot_general(lf.T, tri, nt, precision=hi, preferred_element_type=f32)
        li_rows = gates.T
        diff = (t_i - s_i if d == 0 else s_i - t_i).astype(f32)
        tau = t_col if d == 0 else (CHUNK - 1.0) - t_col
        base = 4 * W_A
        qs, ks, vs, qk, qst = [], [], [], [], []
        for hd in range(H_A):
            j = d * H_A + hd
            q = p_ref[:, hd * DK_A:(hd + 1) * DK_A]
            k = p_ref[:, W_A + hd * DK_A:W_A + (hd + 1) * DK_A] * DK_A ** -0.5
            vb_ = p_ref[:, 2 * W_A + hd * DV_A:2 * W_A + (hd + 1) * DV_A].astype(bf16)
            qb_ = q.astype(bf16)
            qs.append(q), ks.append(k), vs.append(vb_)
            qk.append(lax.dot_general(qb_, k.astype(bf16), nt, preferred_element_type=f32))
            qst.append(jnp.dot(qb_, c_scr[j].astype(bf16), preferred_element_type=f32))
        for hd in range(H_B):
            j = d * H_B + hd
            q = p_ref[:, base + hd * DK_B:base + (hd + 1) * DK_B] * DK_B ** -0.5
            k = p_ref[:, base + W_B + hd * DK_B:base + W_B + (hd + 1) * DK_B]
            vb_ = p_ref[:, base + 2 * W_B + hd * DV_B:base + 2 * W_B + (hd + 1) * DV_B].astype(bf16)
            qb_ = q.astype(bf16)
            qs.append(q), ks.append(k), vs.append(vb_)
            qk.append(lax.dot_general(qb_, k.astype(bf16), nt, preferred_element_type=f32))
            qst.append(jnp.dot(qb_, s_scr[j].astype(bf16), preferred_element_type=f32))
        s_all, kw, post = [], [], []
        for hd in range(H_A):
            j = d * H_A + hd
            b_col = b_cols[:, N_HD + j:N_HD + j + 1]
            b_row = b_rows[N_HD + j:N_HD + j + 1, :]
            li_col = gates[:, j:j + 1]
            li_row = li_rows[j:j + 1, :]
            m_prev = m_scr[j][:, 0:1]
            dlog = jnp.where(mask, b_col - b_row + li_row, -jnp.inf)
            inter = b_col + m_prev
            m_t = jnp.maximum(inter, jnp.max(dlog, axis=1, keepdims=True))
            s = qk[hd] * jnp.exp(dlog - m_t)
            w_int = jnp.exp(inter - m_t)
            den = jnp.sum(s, axis=1, keepdims=True) + w_int * jnp.sum(qs[hd] * n_scr[j], axis=1, keepdims=True)
            b_last = b_col[last:last + 1]
            g_col = b_last - b_col + li_col
            m_new = jnp.maximum(b_last + m_prev, jnp.max(g_col, axis=0, keepdims=True))
            wg = jnp.exp(g_col - m_new)
            dec = jnp.exp(b_last + m_prev - m_new)
            s_all.append(s.astype(bf16))
            kw.append(ks[hd] * wg)
            post.append((w_int, jnp.maximum(jnp.abs(den), jnp.exp(-m_t)), dec, m_new))
        for hd in range(H_B):
            lg = lg_ref[d, hd]
            decay = jnp.where(diff >= 0, jnp.exp(jnp.maximum(diff, 0.0) * lg), 0.0)
            xi = jnp.exp((tau + 1.0) * lg)
            zeta = jnp.exp((CHUNK - 1.0 - tau) * lg)
            g_end = jnp.exp(jnp.full((1, 1), CHUNK, f32) * lg)
            s_all.append((qk[H_A + hd] * decay).astype(bf16))
            kw.append(ks[H_A + hd] * zeta)
            post.append((xi, g_end))
        for hd in range(H_A):
            j = d * H_A + hd
            w_int, den, dec, m_new = post[hd]
            num = jnp.dot(s_all[hd], vs[hd], preferred_element_type=f32) + w_int * qst[hd]
            h_ref[:, hd * DV_A:(hd + 1) * DV_A] = num / den
            c_scr[j] = dec * c_scr[j] + lax.dot_general(kw[hd].astype(bf16), vs[hd], tn, preferred_element_type=f32)
            n_scr[j] = dec * n_scr[j] + jnp.sum(kw[hd], axis=0, keepdims=True)
            m_scr[j] = jnp.broadcast_to(m_new, (1, DK_A))
        for hd in range(H_B):
            j = d * H_B + hd
            xi, g_end = post[H_A + hd]
            o = jnp.dot(s_all[H_A + hd], vs[H_A + hd], preferred_element_type=f32) + xi * qst[H_A + hd]
            h_ref[:, W_A + hd * DV_B:W_A + (hd + 1) * DV_B] = o
            s_scr[j] = g_end * s_scr[j] + lax.dot_general(kw[H_A + hd].astype(bf16), vs[H_A + hd], tn, preferred_element_type=f32)

    @pl.when(ci == pl.num_programs(1) - 1)
    def _():
        cf_ref[0] = c_scr[...]
        nf_ref[0] = n_scr[...]
        mf_ref[0] = m_scr[...]
        sf_ref[0] = s_scr[...]


def _even_scan(proj, n_seq, seq_len, c0, n0, m0, s0, gate_bias, log_gamma):
    nc = seq_len // CHUNK
    f32 = jnp.float32
    st4 = lambda s, c: (s, 0, 0, 0)
    big = pl.BlockSpec((1, N_HD, DK_A, DV_A), st4)
    small = pl.BlockSpec((1, N_HD, 1, DK_A), st4)
    fwd = lambda s, c: (s * nc + c, 0)
    bwd = lambda s, c: (s * nc + (nc - 1 - c), 0)
    n = n_seq * seq_len
    return pl.pallas_call(
        _even_scan_kernel,
        grid=(n_seq, nc),
        in_specs=[pl.BlockSpec((CHUNK, EVEN_COLS), fwd), pl.BlockSpec((CHUNK, EVEN_COLS), bwd), big, small, small, big,
                  pl.BlockSpec((1, LANE), lambda s, c: (0, 0)), pl.BlockSpec(memory_space=pltpu.SMEM)],
        out_specs=[pl.BlockSpec((CHUNK, W_A + W_B), fwd), pl.BlockSpec((CHUNK, W_A + W_B), bwd), big, small, small, big],
        out_shape=[jax.ShapeDtypeStruct((n, W_A + W_B), f32), jax.ShapeDtypeStruct((n, W_A + W_B), f32),
                   jax.ShapeDtypeStruct(c0.shape, f32), jax.ShapeDtypeStruct(n0.shape, f32), jax.ShapeDtypeStruct(m0.shape, f32), jax.ShapeDtypeStruct(s0.shape, f32)],
        scratch_shapes=[pltpu.VMEM((N_HD, DK_A, DV_A), f32), pltpu.VMEM((N_HD, 1, DK_A), f32), pltpu.VMEM((N_HD, 1, DK_A), f32), pltpu.VMEM((N_HD, DK_B, DV_B), f32)],
        name="even_scan",
    )(proj, proj, c0, n0, m0, s0, gate_bias, log_gamma)


def _even_post_kernel(hf_ref, hb_ref, oa_ref, gb_ref, x_ref, g1_ref, gn_ref, w_ref, lng_ref, lnb_ref, o_ref):
    hs = hf_ref[...] + hb_ref[...]
    parts = []
    for hd in range(H_A + H_B):
        hh = hs[:, hd * DV_A:(hd + 1) * DV_A]
        mu = jnp.mean(hh, axis=-1, keepdims=True)
        hc = hh - mu
        var = jnp.mean(hc * hc, axis=-1, keepdims=True)
        parts.append(hc * lax.rsqrt(var + 1e-5))
    normed = jnp.concatenate(parts, axis=1) * gn_ref[...]
    gb = gb_ref[...]
    act = jnp.concatenate([jax.nn.sigmoid(oa_ref[...]), gb * jax.nn.sigmoid(gb)], axis=1)
    ms = jnp.dot((normed * act).astype(jnp.bfloat16), w_ref[...], preferred_element_type=jnp.float32)
    o_ref[...] = _layer_norm_rows(ALPHA * x_ref[...] + g1_ref[0] * ms, lng_ref[...], lnb_ref[...])


def _even_post(hf, hb, proj, x, g1, gn, w_out_b, ln_g, ln_b):
    n_seq, seq_len, d = x.shape
    n = n_seq * seq_len
    per_seq = seq_len // ROW_TILE
    row = lambda i: (i, 0)
    full = lambda i: (0, 0)
    return pl.pallas_call(
        _even_post_kernel,
        grid=(n // ROW_TILE,),
        in_specs=[pl.BlockSpec((ROW_TILE, W_A + W_B), row), pl.BlockSpec((ROW_TILE, W_A + W_B), row),
                  pl.BlockSpec((ROW_TILE, W_A), lambda i: (i, 3)), pl.BlockSpec((ROW_TILE, W_B), lambda i: (i, 7)),
                  pl.BlockSpec((ROW_TILE, d), row), pl.BlockSpec((1, 1, d), lambda i: (i // per_seq, 0, 0)),
                  pl.BlockSpec((1, W_A + W_B), full), pl.BlockSpec((W_A + W_B, d), full), pl.BlockSpec((1, d), full), pl.BlockSpec((1, d), full)],
        out_specs=pl.BlockSpec((ROW_TILE, d), row),
        out_shape=jax.ShapeDtypeStruct((n, d), jnp.float32),
        name="even_post",
    )(hf, hb, proj, proj, x.reshape(n, d), g1, gn, w_out_b, ln_g, ln_b).reshape(x.shape)


def _even_weights(w_in, i_bias, f_bias, log_decay, gn_a, gn_b, w_out):
    qa, ka, va, oa, ia, fa, qb, kb, vb, gb = jnp.split(w_in, EVEN_SPLITS, axis=1)
    gates = jnp.pad(jnp.concatenate([ia, fa], axis=1), ((0, 0), (0, LANE - 2 * N_HD)))
    w_in_b = jnp.concatenate([qa, ka, va, oa, qb, kb, vb, gb, gates], axis=1).astype(jnp.bfloat16)
    gate_bias = jnp.pad(jnp.concatenate([i_bias.reshape(-1), f_bias.reshape(-1)]), (0, LANE - 2 * N_HD)).reshape(1, LANE)
    log_gamma = -jnp.exp(log_decay.astype(jnp.float32))
    gn = jnp.concatenate([gn_a.reshape(-1), gn_b.reshape(-1)]).reshape(1, W_A + W_B)
    return w_in_b, gate_bias, log_gamma, gn, w_out.astype(jnp.bfloat16)


def _even_layer(x, sc1, sh1, g1, wts, ln_g, ln_b, c0, n0, m0, s0):
    w_in_b, gate_bias, log_gamma, gn, w_out_b = wts
    n_seq, seq_len, d = x.shape
    proj = _mod_proj(x, sc1, sh1, w_in_b)
    st = lambda a: a.reshape(n_seq, N_HD, DK_A, DV_A)
    vec = lambda a: a.reshape(n_seq, N_HD, 1, DK_A)
    m0b = jnp.broadcast_to(m0.reshape(n_seq, N_HD, 1, 1), (n_seq, N_HD, 1, DK_A))
    hf, hb, cf, nf, mf, sf = _even_scan(proj, n_seq, seq_len, st(c0), vec(n0), m0b, st(s0), gate_bias, log_gamma)
    x_new = _even_post(hf, hb, proj, x, g1, gn, w_out_b, ln_g.reshape(1, d), ln_b.reshape(1, d))
    fin = (cf.reshape(n_seq, N_DIR, H_A, DK_A, DV_A), nf.reshape(n_seq, N_DIR, H_A, DK_A), mf[:, :, 0, 0].reshape(n_seq, N_DIR, H_A), sf.reshape(n_seq, N_DIR, H_B, DK_B, DV_B))
    return x_new, fin


ODD_DT = D_INNER + CONV_DIM
ODD_COLS = ODD_DT + LANE
CONV_TILE = 1024
HALO = 8
GW = J_C * P_C


def _conv_kernel(x_ref, prev_ref, next_ref, w_ref, b_ref, o_ref, *, per_seq):
    i = pl.program_id(0)
    pad = D_CONV // 2
    rows = x_ref.shape[0]
    first = lax.rem(i, per_seq) == 0
    last = lax.rem(i, per_seq) == per_seq - 1
    prev = jnp.where(first, 0.0, prev_ref[HALO - pad:, :])
    nxt = jnp.where(last, 0.0, next_ref[:pad, :])
    ext = jnp.concatenate([prev, x_ref[...], nxt], axis=0)
    acc = b_ref[...] + ext[0:rows] * w_ref[0:1, :]
    for k in range(1, D_CONV):
        acc = acc + ext[k:k + rows] * w_ref[k:k + 1, :]
    o_ref[...] = acc * jax.nn.sigmoid(acc)


def _conv_silu(proj, seq_len, conv_w, conv_b):
    n = proj.shape[0]
    per_seq = seq_len // ROW_TILE
    c0 = D_INNER // CONV_TILE
    hb = ROW_TILE // HALO
    n_hb = n // HALO
    return pl.pallas_call(
        functools.partial(_conv_kernel, per_seq=per_seq),
        grid=(n // ROW_TILE, CONV_DIM // CONV_TILE),
        in_specs=[pl.BlockSpec((ROW_TILE, CONV_TILE), lambda i, j: (i, c0 + j)),
                  pl.BlockSpec((HALO, CONV_TILE), lambda i, j: (jnp.maximum(i * hb - 1, 0), c0 + j)),
                  pl.BlockSpec((HALO, CONV_TILE), lambda i, j: (jnp.minimum((i + 1) * hb, n_hb - 1), c0 + j)),
                  pl.BlockSpec((D_CONV, CONV_TILE), lambda i, j: (0, j)), pl.BlockSpec((1, CONV_TILE), lambda i, j: (0, j))],
        out_specs=pl.BlockSpec((ROW_TILE, CONV_TILE), lambda i, j: (i, j)),
        out_shape=jax.ShapeDtypeStruct((n, CONV_DIM), jnp.float32),
        name="conv_silu",
    )(proj, proj, proj, conv_w, conv_b.reshape(1, CONV_DIM))


def _spread_heads(mat, col0):
    lane = lax.broadcasted_iota(jnp.int32, (mat.shape[0], LANE), 1)
    per_tile = LANE // P_C
    tiles = []
    for j in range(0, J_C, per_tile):
        t = mat[:, col0 + j:col0 + j + 1]
        for r in range(1, per_tile):
            t = jnp.where(lane < r * P_C, t, mat[:, col0 + j + r:col0 + j + r + 1])
        tiles.append(jnp.broadcast_to(t, (mat.shape[0], LANE)))
    return jnp.concatenate(tiles, axis=1)


def _ssd_scan_kernel(af_ref, ab_ref, dtf_ref, dtb_ref, bias_ref, alog_ref, h0_ref, yf_ref, yb_ref, hf_ref, h_scr):
    f32, bf16 = jnp.float32, jnp.bfloat16
    ci = pl.program_id(1)

    @pl.when(ci == 0)
    def _():
        h_scr[...] = h0_ref[0]

    t_i = lax.broadcasted_iota(jnp.int32, (CHUNK, CHUNK), 0)
    s_i = lax.broadcasted_iota(jnp.int32, (CHUNK, CHUNK), 1)
    nt = (((1,), (1,)), ((), ()))
    tn = (((0,), (0,)), ((), ()))
    hi = lax.Precision.HIGHEST
    a_neg = -jnp.exp(alog_ref[...])

    for d, (x_ref, dt_ref, y_ref) in enumerate(((af_ref, dtf_ref, yf_ref), (ab_ref, dtb_ref, yb_ref))):
        mask = (s_i <= t_i) if d == 0 else (s_i >= t_i)
        tri = mask.astype(f32)
        last = CHUNK - 1 if d == 0 else 0
        dtv = jax.nn.softplus(dt_ref[...] + bias_ref[...])
        a = dtv * a_neg
        ac_cols = jnp.dot(tri, a, precision=hi, preferred_element_type=f32)
        ac_rows = lax.dot_general(a.T, tri, nt, precision=hi, preferred_element_type=f32)
        dt_rows = dtv.T
        e_cols = jnp.exp(ac_cols)
        w_cols = jnp.exp(ac_cols[last:last + 1] - ac_cols) * dtv
        bgs, xgs, cbs, inters = [], [], [], []
        for g in range(G_C):
            bg = x_ref[:, D_INNER + g * N_C:D_INNER + (g + 1) * N_C].astype(bf16)
            cg = x_ref[:, D_INNER + G_C * N_C + g * N_C:D_INNER + G_C * N_C + (g + 1) * N_C].astype(bf16)
            bgs.append(bg)
            xgs.append(x_ref[:, g * GW:(g + 1) * GW])
            cbs.append(lax.dot_general(cg, bg, nt, preferred_element_type=f32))
            inters.append(jnp.dot(cg, h_scr[d, g].astype(bf16), preferred_element_type=f32))
        s_all = []
        for g in range(G_C):
            for j in range(J_C):
                col = d * H_C + g * J_C + j
                seg = jnp.where(mask, ac_cols[:, col:col + 1] - ac_rows[col:col + 1, :], -jnp.inf)
                s_all.append((cbs[g] * jnp.exp(seg) * dt_rows[col:col + 1, :]).astype(bf16))
        for g in range(G_C):
            col0 = d * H_C + g * J_C
            e_g = _spread_heads(e_cols, col0)
            intra = [jnp.dot(s_all[g * J_C + j], xgs[g][:, j * P_C:(j + 1) * P_C].astype(bf16), preferred_element_type=f32) for j in range(J_C)]
            y_ref[:, g * GW:(g + 1) * GW] = jnp.concatenate(intra, axis=1) + e_g * inters[g]
            xw = (xgs[g] * _spread_heads(w_cols, col0)).astype(bf16)
            h_scr[d, g] = e_g[last:last + 1] * h_scr[d, g] + lax.dot_general(bgs[g], xw, tn, preferred_element_type=f32)

    @pl.when(ci == pl.num_programs(1) - 1)
    def _():
        hf_ref[0] = h_scr[...]


def _ssd_chunk_scan(xa, proj, n_seq, seq_len, dt_bias_row, a_log_row, h0):
    nc = seq_len // CHUNK
    n = n_seq * seq_len
    f32 = jnp.float32
    fwd = lambda s, c: (s * nc + c, 0)
    bwd = lambda s, c: (s * nc + (nc - 1 - c), 0)
    dcol = ODD_DT // LANE
    st = pl.BlockSpec((1, N_DIR, G_C, N_C, GW), lambda s, c: (s, 0, 0, 0, 0))
    row = pl.BlockSpec((1, LANE), lambda s, c: (0, 0))
    return pl.pallas_call(
        _ssd_scan_kernel,
        grid=(n_seq, nc),
        in_specs=[pl.BlockSpec((CHUNK, CONV_DIM), fwd), pl.BlockSpec((CHUNK, CONV_DIM), bwd),
                  pl.BlockSpec((CHUNK, LANE), lambda s, c: (s * nc + c, dcol)), pl.BlockSpec((CHUNK, LANE), lambda s, c: (s * nc + (nc - 1 - c), dcol)),
                  row, row, st],
        out_specs=[pl.BlockSpec((CHUNK, D_INNER), fwd), pl.BlockSpec((CHUNK, D_INNER), bwd), st],
        out_shape=[jax.ShapeDtypeStruct((n, D_INNER), f32), jax.ShapeDtypeStruct((n, D_INNER), f32), jax.ShapeDtypeStruct(h0.shape, f32)],
        scratch_shapes=[pltpu.VMEM((N_DIR, G_C, N_C, GW), f32)],
        name="ssd_scan",
    )(xa, xa, proj, proj, dt_bias_row, a_log_row, h0)


def _odd_post_kernel(yf_ref, yb_ref, xs_ref, z_ref, x_ref, g1_ref, dsk_ref, nw_ref, w_ref, lng_ref, lnb_ref, o_ref):
    z = z_ref[...]
    y = (yf_ref[...] + yb_ref[...] + dsk_ref[...] * xs_ref[...]) * (z * jax.nn.sigmoid(z))
    y = y * lax.rsqrt(jnp.mean(y * y, axis=-1, keepdims=True) + 1e-5) * nw_ref[...]
    ms = jnp.dot(y.astype(jnp.bfloat16), w_ref[...], preferred_element_type=jnp.float32)
    o_ref[...] = _layer_norm_rows(ALPHA * x_ref[...] + g1_ref[0] * ms, lng_ref[...], lnb_ref[...])


def _odd_post(yf, yb, xa, proj, x, g1, d_row, norm_w, w_out_b, ln_g, ln_b):
    n_seq, seq_len, d = x.shape
    n = n_seq * seq_len
    per_seq = seq_len // ROW_TILE
    row = lambda i: (i, 0)
    full = lambda i: (0, 0)
    wide = pl.BlockSpec((ROW_TILE, D_INNER), row)
    return pl.pallas_call(
        _odd_post_kernel,
        grid=(n // ROW_TILE,),
        in_specs=[wide, wide, wide, wide, pl.BlockSpec((ROW_TILE, d), row), pl.BlockSpec((1, 1, d), lambda i: (i // per_seq, 0, 0)),
                  pl.BlockSpec((1, D_INNER), full), pl.BlockSpec((1, D_INNER), full), pl.BlockSpec((D_INNER, d), full), pl.BlockSpec((1, d), full), pl.BlockSpec((1, d), full)],
        out_specs=pl.BlockSpec((ROW_TILE, d), row),
        out_shape=jax.ShapeDtypeStruct((n, d), jnp.float32),
        name="odd_post",
    )(yf, yb, xa, proj, x.reshape(n, d), g1, d_row, norm_w, w_out_b, ln_g, ln_b).reshape(x.shape)


def _odd_weights(w_in, dt_bias, a_log, d_skip, norm_w, w_out):
    w_in_b = jnp.pad(w_in, ((0, 0), (0, ODD_COLS - w_in.shape[1]))).astype(jnp.bfloat16)
    lane_row = lambda a: jnp.pad(a.astype(jnp.float32).reshape(-1), (0, LANE - N_DIR * H_C)).reshape(1, LANE)
    d_row = jnp.repeat(d_skip.astype(jnp.float32), P_C).reshape(1, D_INNER)
    return w_in_b, lane_row(dt_bias), lane_row(a_log), d_row, norm_w.reshape(1, D_INNER), w_out.astype(jnp.bfloat16)


def _odd_layer(x, sc1, sh1, g1, wts, conv_w, conv_b, ln_g, ln_b, h0):
    w_in_b, dt_row, alog_row, d_row, nw_row, w_out_b = wts
    n_seq, seq_len, d = x.shape
    proj = _mod_proj(x, sc1, sh1, w_in_b)
    xa = _conv_silu(proj, seq_len, conv_w, conv_b)
    to_scan = lambda h: h.reshape(n_seq, N_DIR, G_C, J_C, P_C, N_C).transpose(0, 1, 2, 5, 3, 4).reshape(n_seq, N_DIR, G_C, N_C, GW)
    yf, yb, hfin = _ssd_chunk_scan(xa, proj, n_seq, seq_len, dt_row, alog_row, to_scan(h0))
    x_new = _odd_post(yf, yb, xa, proj, x, g1, d_row, nw_row, w_out_b, ln_g.reshape(1, d), ln_b.reshape(1, d))
    fin = hfin.reshape(n_seq, N_DIR, G_C, N_C, J_C, P_C).transpose(0, 1, 2, 4, 5, 3).reshape(n_seq, N_DIR, H_C, P_C, N_C)
    return x_new, fin


def _modulation(cvec, w, b):
    m = jax.nn.silu(cvec) @ w + b
    return [t[:, None, :] for t in jnp.split(m, 6, axis=-1)]


def _grid_pos_embed(n_tok, dtype):
    rows = n_tok // GRID_W
    r = jnp.repeat(jnp.arange(rows, dtype=jnp.float32), GRID_W)
    col = jnp.tile(jnp.arange(GRID_W, dtype=jnp.float32), rows)
    quarter = D_MODEL // 4
    freqs = 1.0 / (10000.0 ** (jnp.arange(quarter, dtype=jnp.float32) / quarter))
    er = r[:, None] * freqs
    ec = col[:, None] * freqs
    return jnp.concatenate([jnp.sin(er), jnp.cos(er), jnp.sin(ec), jnp.cos(ec)], axis=-1).astype(dtype)


def _expert_slab_copy(tab_ref, buf_ref, sem_ref, row, pair, slot):
    src = tab_ref.at[pl.ds(pl.multiple_of(row, SLAB_ROWS), SLAB_ROWS), :]
    return pltpu.make_async_copy(src, buf_ref.at[slot, pl.ds(pair * SLAB_PITCH, SLAB_ROWS), :], sem_ref.at[slot])


def _peer_gather_kernel(idx_ref, x_ref, g_ref, tab_ref, o_ref, buf_ref, sem_ref):
    n_tok = x_ref.shape[0]
    n_chunk = x_ref.shape[1] // LANE

    per_phase = PEER_PAIRS // (2 * n_chunk)

    def issue(t, slot, p0, p1):
        for p in range(p0, p1):
            _expert_slab_copy(tab_ref, buf_ref, sem_ref, idx_ref[t, p], p, slot).start(priority=p % 2)

    def wait(slot):
        n_rows = PEER_PAIRS * SLAB_ROWS
        pltpu.make_async_copy(tab_ref.at[pl.ds(0, n_rows), :], buf_ref.at[slot, pl.ds(0, n_rows), :], sem_ref.at[slot]).wait()

    def chunk(slot, c):
        return buf_ref[slot, pl.ds(c, PEER_PAIRS, stride=SLAB_PITCH), :]

    eye = lax.broadcasted_iota(jnp.int32, (PEER_PAIRS, PEER_PAIRS), 0) == lax.broadcasted_iota(jnp.int32, (PEER_PAIRS, PEER_PAIRS), 1)

    def compute(t, slot, t_next, slot_next):
        phase = [0]

        def issue_some():
            if t_next is not None:
                issue(t_next, slot_next, phase[0] * per_phase, (phase[0] + 1) * per_phase)
            phase[0] += 1

        wait(slot)
        gt = g_ref[pl.ds(t, 1), :]
        xt = x_ref[pl.ds(t, 1), :]
        acc = chunk(slot, 0) * xt[:, :LANE]
        issue_some()
        for c in range(1, n_chunk):
            acc = acc + chunk(slot, c) * xt[:, c * LANE:(c + 1) * LANE]
            issue_some()
        act = jnp.sum(acc, axis=1, keepdims=True)
        g_col = jnp.sum(jnp.where(eye, gt, 0.0), axis=1, keepdims=True)
        w = g_col * (0.5 * act * (1.0 + lax.erf(act * np.float32(np.sqrt(0.5)))))
        outs = []
        for c in range(n_chunk):
            outs.append(jnp.sum(w * chunk(slot, n_chunk + c), axis=0, keepdims=True))
            issue_some()
        o_ref[pl.ds(t, 1), :] = jnp.concatenate(outs, axis=1)

    ahead = PEER_NBUF - 1
    for s in range(ahead):
        issue(s, s, 0, PEER_PAIRS)

    def group(base, n_issue):
        for j in range(PEER_NBUF):
            if j < n_issue:
                compute(base + j, j, base + j + ahead, (j + ahead) % PEER_NBUF)
            else:
                compute(base + j, j, None, None)

    n_group = n_tok // PEER_NBUF

    def steady(gi, carry):
        group(gi * PEER_NBUF, PEER_NBUF)
        return carry

    lax.fori_loop(0, n_group - 1, steady, 0)
    group((n_group - 1) * PEER_NBUF, PEER_NBUF - ahead)


def _peer_gather(x, idx, g, uv_tab):
    n, d = x.shape
    return pl.pallas_call(
        _peer_gather_kernel,
        grid=(n // PEER_TB,),
        in_specs=[
            pl.BlockSpec((PEER_TB, PEER_PAIRS), lambda i: (i, 0), memory_space=pltpu.SMEM),
            pl.BlockSpec((PEER_TB, d), lambda i: (i, 0)),
            pl.BlockSpec((PEER_TB, PEER_PAIRS), lambda i: (i, 0)),
            pl.BlockSpec(memory_space=pl.ANY),
        ],
        out_specs=pl.BlockSpec((PEER_TB, d), lambda i: (i, 0)),
        out_shape=jax.ShapeDtypeStruct((n, d), jnp.float32),
        scratch_shapes=[pltpu.VMEM((PEER_NBUF, PEER_PAIRS * SLAB_PITCH, LANE), jnp.float32), pltpu.SemaphoreType.DMA((PEER_NBUF,))],
        name="peer_gather",
    )(idx * SLAB_ROWS, x, g, uv_tab)


def _top_rows(s, k, payload=None):
    (vals, sel), = _top_rows_many([s], k, None if payload is None else [payload])
    return vals, sel


def _top_rows_many(ss, k, payloads=None):
    n_rows, width = ss[0].shape
    row = lax.broadcasted_iota(jnp.int32, (n_rows, width), 0)
    out_row = lax.broadcasted_iota(jnp.int32, (k, width), 0)
    ss = list(ss)
    vals = [jnp.zeros((k, width), jnp.float32) for _ in ss]
    sel = [jnp.zeros((k, width), jnp.int32) for _ in ss]
    for j in range(k):
        for i in range(len(ss)):
            m = jnp.max(ss[i], axis=0, keepdims=True)
            r = jnp.min(jnp.where(ss[i] == m, row, n_rows), axis=0, keepdims=True)
            hit = row == r
            ss[i] = jnp.where(hit, -jnp.inf, ss[i])
            what = r if payloads is None else jnp.max(jnp.where(hit, payloads[i], -1), axis=0, keepdims=True)
            vals[i] = jnp.where(out_row == j, m, vals[i])
            sel[i] = jnp.where(out_row == j, what, sel[i])
    return list(zip(vals, sel))


def _peer_select_kernel(x_ref, sc_ref, sh_ref, wq_ref, keys_ref, xm_ref, idx_ref, g_ref, q_scr, sv_scr, si_scr, bi_scr, bg_scr):
    f32 = jnp.float32
    n_hc = 2 * H_P
    h = x_ref[...] * (1.0 + sc_ref[0]) + sh_ref[0]
    xm_ref[...] = h
    q = jnp.dot(h.astype(jnp.bfloat16), wq_ref[...], preferred_element_type=f32)
    for hc in range(n_hc):
        q_scr[hc] = q[:, hc * D_HALF:(hc + 1) * D_HALF].astype(jnp.bfloat16)

    def sub_key_top(hd, carry):
        hcs = (2 * hd, 2 * hd + 1)
        ss = [lax.dot_general(keys_ref[hc], q_scr[hc], (((1,), (1,)), ((), ())), preferred_element_type=f32) for hc in hcs]
        for hc, (v, i) in zip(hcs, _top_rows_many(ss, TOPK)):
            sv_scr[hc], si_scr[hc] = v, i
        return carry

    lax.fori_loop(0, H_P, sub_key_top, 0)

    def pair_top(hd, carry):
        sv0, sv1 = sv_scr[2 * hd], sv_scr[2 * hd + 1]
        si0, si1 = si_scr[2 * hd], si_scr[2 * hd + 1]
        half = TOPK // 2
        cand, cidx, n_cand = [], [], 0
        for a in range(half):
            n_b = TOPK // (a + 1)
            cand.append(sv0[a:a + 1] + sv1[:n_b])
            cidx.append(si0[a:a + 1] * N_KEYS + si1[:n_b])
            n_cand += n_b
        n_fill = -n_cand % 8
        cand.append(jnp.full((n_fill, sv0.shape[1]), -jnp.inf, f32))
        cidx.append(jnp.full((n_fill, sv0.shape[1]), -1, jnp.int32))
        cand.append(sv0[half:] + sv1[0:1])
        cidx.append(si0[half:] * N_KEYS + si1[0:1])
        best, eid = _top_rows(jnp.concatenate(cand, axis=0), TOPK, payload=jnp.concatenate(cidx, axis=0))
        e = jnp.exp(best - best[0:1])
        bi_scr[hd] = eid
        bg_scr[hd] = e / jnp.sum(e, axis=0, keepdims=True)
        return carry

    lax.fori_loop(0, H_P, pair_top, 0)
    tm = x_ref.shape[0]
    idx_ref[...] = bi_scr[...].reshape(PEER_PAIRS, tm).T
    g_ref[...] = bg_scr[...].reshape(PEER_PAIRS, tm).T


def _peer_select(x, sc, sh, w_q, sub_keys):
    n_seq, seq_len, d = x.shape
    n = n_seq * seq_len
    n_hc = 2 * H_P
    per_seq = seq_len // SEL_TM
    tok = lambda i: (i, 0)
    seq = lambda i: (i // per_seq, 0, 0)
    return pl.pallas_call(
        _peer_select_kernel,
        grid=(n // SEL_TM,),
        in_specs=[
            pl.BlockSpec((SEL_TM, d), tok),
            pl.BlockSpec((1, 1, d), seq),
            pl.BlockSpec((1, 1, d), seq),
            pl.BlockSpec((d, H_P * D_QK), lambda i: (0, 0)),
            pl.BlockSpec((n_hc, N_KEYS, D_HALF), lambda i: (0, 0, 0)),
        ],
        out_specs=[pl.BlockSpec((SEL_TM, d), tok), pl.BlockSpec((SEL_TM, PEER_PAIRS), tok), pl.BlockSpec((SEL_TM, PEER_PAIRS), tok)],
        out_shape=[jax.ShapeDtypeStruct((n, d), jnp.float32), jax.ShapeDtypeStruct((n, PEER_PAIRS), jnp.int32), jax.ShapeDtypeStruct((n, PEER_PAIRS), jnp.float32)],
        scratch_shapes=[
            pltpu.VMEM((n_hc, SEL_TM, D_HALF), jnp.bfloat16),
            pltpu.VMEM((n_hc, TOPK, SEL_TM), jnp.float32),
            pltpu.VMEM((n_hc, TOPK, SEL_TM), jnp.int32),
            pltpu.VMEM((H_P, TOPK, SEL_TM), jnp.int32),
            pltpu.VMEM((H_P, TOPK, SEL_TM), jnp.float32),
        ],
        name="peer_select",
    )(x.reshape(n, d), sc, sh, w_q.astype(jnp.bfloat16), sub_keys.reshape(n_hc, N_KEYS, D_HALF).astype(jnp.bfloat16))


def _residual_norm_kernel(x_ref, f_ref, g_ref, lng_ref, lnb_ref, o_ref):
    o_ref[...] = _layer_norm_rows(ALPHA * x_ref[...] + g_ref[0] * f_ref[...], lng_ref[...], lnb_ref[...])


def _residual_norm(x, f, gate, ln_g, ln_b):
    n_seq, seq_len, d = x.shape
    n = n_seq * seq_len
    per_seq = seq_len // ROW_TILE
    row = pl.BlockSpec((ROW_TILE, d), lambda i: (i, 0))
    vec = pl.BlockSpec((1, d), lambda i: (0, 0))
    return pl.pallas_call(
        _residual_norm_kernel,
        grid=(n // ROW_TILE,),
        in_specs=[row, row, pl.BlockSpec((1, 1, d), lambda i: (i // per_seq, 0, 0)), vec, vec],
        out_specs=row,
        out_shape=jax.ShapeDtypeStruct((n, d), jnp.float32),
        name="residual_norm",
    )(x.reshape(n, d), f.reshape(n, d), gate, ln_g.reshape(1, d), ln_b.reshape(1, d)).reshape(x.shape)


def _peer(x, sc, sh, w_q, sub_keys, uv_tab):
    xm, idx, g = _peer_select(x, sc, sh, w_q, sub_keys)
    return _peer_gather(xm, idx, g, uv_tab).reshape(x.shape)


def kernel(x_prompt, x_sample, c, c_ctx, state_mlstm_C, state_mlstm_n, state_mlstm_m, state_ret, state_ssd, w_mod, b_mod, ln1_g, ln1_b, ln2_g, ln2_b, even_w_in, mlstm_i_bias, mlstm_f_bias, ret_log_decay, even_gn_a, even_gn_b, even_w_out, odd_w_in, odd_conv_w, odd_conv_b, odd_dt_bias, odd_a_log, odd_d, odd_norm_w, odd_w_out, peer_w_q, peer_sub_keys, peer_u, peer_v):
    f32 = jnp.float32
    bp = x_prompt.shape[0]
    xp = x_prompt
    xs = x_sample + _grid_pos_embed(x_sample.shape[1], x_sample.dtype)[None]
    new_C, new_n, new_m, new_S, new_h = [], [], [], [], []
    bcast = lambda m: jnp.broadcast_to(m, (bp, 1, D_MODEL))
    for l in range(DEPTH):
        sh1p, sc1p, g1p, sh2p, sc2p, g2p = [bcast(m) for m in _modulation(c_ctx[None, :], w_mod[l], b_mod[l])]
        sh1s, sc1s, g1s, sh2s, sc2s, g2s = _modulation(c, w_mod[l], b_mod[l])
        j = l // 2
        if l % 2 == 0:
            wts = _even_weights(even_w_in[j], mlstm_i_bias[j], mlstm_f_bias[j], ret_log_decay[j], even_gn_a[j], even_gn_b[j], even_w_out[j])
            zero = (jnp.zeros((bp, N_DIR, H_A, DK_A, DV_A), f32), jnp.zeros((bp, N_DIR, H_A, DK_A), f32), jnp.zeros((bp, N_DIR, H_A), f32), jnp.zeros((bp, N_DIR, H_B, DK_B, DV_B), f32))
            xp, fin = _even_layer(xp, sc1p, sh1p, g1p, wts, ln1_g[l], ln1_b[l], *zero)
            xs, _ = _even_layer(xs, sc1s, sh1s, g1s, wts, ln1_g[l], ln1_b[l], state_mlstm_C[:, j], state_mlstm_n[:, j], state_mlstm_m[:, j], state_ret[:, j])
            new_C.append(fin[0])
            new_n.append(fin[1])
            new_m.append(fin[2])
            new_S.append(fin[3])
        else:
            wts = _odd_weights(odd_w_in[j], odd_dt_bias[j], odd_a_log[j], odd_d[j], odd_norm_w[j], odd_w_out[j])
            xp, fin = _odd_layer(xp, sc1p, sh1p, g1p, wts, odd_conv_w[j], odd_conv_b[j], ln1_g[l], ln1_b[l], jnp.zeros((bp, N_DIR, H_C, P_C, N_C), f32))
            xs, _ = _odd_layer(xs, sc1s, sh1s, g1s, wts, odd_conv_w[j], odd_conv_b[j], ln1_g[l], ln1_b[l], state_ssd[:, j])
            new_h.append(fin)
        uv_tab = jnp.concatenate([peer_u[l], peer_v[l]], axis=1).reshape(-1, LANE)
        fp = _peer(xp, sc2p, sh2p, peer_w_q[l], peer_sub_keys[l], uv_tab)
        fs = _peer(xs, sc2s, sh2s, peer_w_q[l], peer_sub_keys[l], uv_tab)
        xp = _residual_norm(xp, fp, g2p, ln2_g[l], ln2_b[l])
        xs = _residual_norm(xs, fs, g2s, ln2_g[l], ln2_b[l])
    return (xp, xs, jnp.stack(new_C, axis=1), jnp.stack(new_n, axis=1), jnp.stack(new_m, axis=1), jnp.stack(new_S, axis=1), jnp.stack(new_h, axis=1))
```

```python
import functools

import jax
import jax.numpy as jnp
import numpy as np
from jax import lax
from jax.experimental import pallas as pl
from jax.experimental.pallas import tpu as pltpu

D_MODEL = 1024
DEPTH = 4
GRID_W = 64
CHUNK = 64
N_DIR = 2
ALPHA = (2.0 * DEPTH) ** 0.25
H_A = 4
DK_A = 128
DV_A = 128
H_B = 4
DK_B = 128
DV_B = 128
W_A = H_A * DV_A
W_B = H_B * DV_B
EVEN_SIZES = (H_A * DK_A, H_A * DK_A, W_A, W_A, N_DIR * H_A, N_DIR * H_A, H_B * DK_B, H_B * DK_B, W_B, W_B)
EVEN_SPLITS = [int(s) for s in np.cumsum(EVEN_SIZES)[:-1]]
D_INNER = 2 * D_MODEL
P_C = 64
H_C = D_INNER // P_C
G_C = 4
J_C = H_C // G_C
N_C = 128
D_CONV = 5
CONV_DIM = D_INNER + 2 * G_C * N_C
N_KEYS = 128
H_P = 8
D_QK = 256
D_HALF = D_QK // 2
TOPK = 16

LANE = 128
ROW_TILE = 256
PEER_PAIRS = H_P * TOPK
SEL_TM = 256
PEER_TB = 256
PEER_NBUF = 8
SLAB_ROWS = 2 * D_MODEL // LANE
SLAB_PITCH = SLAB_ROWS + 4


def _mod_proj_kernel(x_ref, sc_ref, sh_ref, w_ref, o_ref):
    h = x_ref[...] * (1.0 + sc_ref[0]) + sh_ref[0]
    o_ref[...] = jnp.dot(h.astype(jnp.bfloat16), w_ref[...], preferred_element_type=jnp.float32)


def _mod_proj(x, sc, sh, wb):
    n_seq, seq_len, d = x.shape
    n = n_seq * seq_len
    n_out = wb.shape[1]
    per_seq = seq_len // ROW_TILE
    seq = lambda i: (i // per_seq, 0, 0)
    return pl.pallas_call(
        _mod_proj_kernel,
        grid=(n // ROW_TILE,),
        in_specs=[pl.BlockSpec((ROW_TILE, d), lambda i: (i, 0)), pl.BlockSpec((1, 1, d), seq), pl.BlockSpec((1, 1, d), seq), pl.BlockSpec((d, n_out), lambda i: (0, 0))],
        out_specs=pl.BlockSpec((ROW_TILE, n_out), lambda i: (i, 0)),
        out_shape=jax.ShapeDtypeStruct((n, n_out), jnp.float32),
        name="mod_proj",
    )(x.reshape(n, d), sc, sh, wb)


def _layer_norm_rows(y, g, b, eps=1e-5):
    mu = jnp.mean(y, axis=-1, keepdims=True)
    yc = y - mu
    var = jnp.mean(yc * yc, axis=-1, keepdims=True)
    return yc * lax.rsqrt(var + eps) * g + b


EVEN_MAIN = 8 * W_A
EVEN_COLS = EVEN_MAIN + LANE
N_HD = N_DIR * H_A


def _even_scan_kernel(pf_ref, pb_ref, c0_ref, n0_ref, m0_ref, s0_ref, gb_ref, lg_ref,
                      hf_ref, hb_ref, cf_ref, nf_ref, mf_ref, sf_ref, c_scr, n_scr, m_scr, s_scr):
    f32, bf16 = jnp.float32, jnp.bfloat16
    ci = pl.program_id(1)

    @pl.when(ci == 0)
    def _():
        c_scr[...] = c0_ref[0]
        n_scr[...] = n0_ref[0]
        m_scr[...] = m0_ref[0]
        s_scr[...] = s0_ref[0]

    t_i = lax.broadcasted_iota(jnp.int32, (CHUNK, CHUNK), 0)
    s_i = lax.broadcasted_iota(jnp.int32, (CHUNK, CHUNK), 1)
    t_col = lax.broadcasted_iota(jnp.int32, (CHUNK, 1), 0).astype(f32)
    nt = (((1,), (1,)), ((), ()))
    tn = (((0,), (0,)), ((), ()))
    hi = lax.Precision.HIGHEST

    for d, (p_ref, h_ref) in enumerate(((pf_ref, hf_ref), (pb_ref, hb_ref))):
        mask = (s_i <= t_i) if d == 0 else (s_i >= t_i)
        tri = mask.astype(f32)
        last = CHUNK - 1 if d == 0 else 0
        gates = p_ref[:, EVEN_MAIN:] + gb_ref[...]
        lf = jax.nn.log_sigmoid(gates)
        b_cols = jnp.dot(tri, lf, precision=hi, preferred_element_type=f32)
        b_rows = lax.dot_general(lf.T, tri, nt, precision=hi, preferred_element_type=f32)
        li_rows = gates.T
        diff = (t_i - s_i if d == 0 else s_i - t_i).astype(f32)
        tau = t_col if d == 0 else (CHUNK - 1.0) - t_col
        base = 4 * W_A
        qs, ks, vs, qk, qst = [], [], [], [], []
        for hd in range(H_A):
            j = d * H_A + hd
            q = p_ref[:, hd * DK_A:(hd + 1) * DK_A]
            k = p_ref[:, W_A + hd * DK_A:W_A + (hd + 1) * DK_A] * DK_A ** -0.5
            vb_ = p_ref[:, 2 * W_A + hd * DV_A:2 * W_A + (hd + 1) * DV_A].astype(bf16)
            qb_ = q.astype(bf16)
            qs.append(q), ks.append(k), vs.append(vb_)
            qk.append(lax.dot_general(qb_, k.astype(bf16), nt, preferred_element_type=f32))
            qst.append(jnp.dot(qb_, c_scr[j].astype(bf16), preferred_element_type=f32))
        for hd in range(H_B):
            j = d * H_B + hd
            q = p_ref[:, base + hd * DK_B:base + (hd + 1) * DK_B] * DK_B ** -0.5
            k = p_ref[:, base + W_B + hd * DK_B:base + W_B + (hd + 1) * DK_B]
            vb_ = p_ref[:, base + 2 * W_B + hd * DV_B:base + 2 * W_B + (hd + 1) * DV_B].astype(bf16)
            qb_ = q.astype(bf16)
            qs.append(q), ks.append(k), vs.append(vb_)
            qk.append(lax.dot_general(qb_, k.astype(bf16), nt, preferred_element_type=f32))
            qst.append(jnp.dot(qb_, s_scr[j].astype(bf16), preferred_element_type=f32))
        s_all, kw, post = [], [], []
        for hd in range(H_A):
            j = d * H_A + hd
            b_col = b_cols[:, N_HD + j:N_HD + j + 1]
            b_row = b_rows[N_HD + j:N_HD + j + 1, :]
            li_col = gates[:, j:j + 1]
            li_row = li_rows[j:j + 1, :]
            m_prev = m_scr[j][:, 0:1]
            dlog = jnp.where(mask, b_col - b_row + li_row, -jnp.inf)
            inter = b_col + m_prev
            m_t = jnp.maximum(inter, jnp.max(dlog, axis=1, keepdims=True))
            s = qk[hd] * jnp.exp(dlog - m_t)
            w_int = jnp.exp(inter - m_t)
            den = jnp.sum(s, axis=1, keepdims=True) + w_int * jnp.sum(qs[hd] * n_scr[j], axis=1, keepdims=True)
            b_last = b_col[last:last + 1]
            g_col = b_last - b_col + li_col
            m_new = jnp.maximum(b_last + m_prev, jnp.max(g_col, axis=0, keepdims=True))
            wg = jnp.exp(g_col - m_new)
            dec = jnp.exp(b_last + m_prev - m_new)
            s_all.append(s.astype(bf16))
            kw.append(ks[hd] * wg)
            post.append((w_int, jnp.maximum(jnp.abs(den), jnp.exp(-m_t)), dec, m_new))
        for hd in range(H_B):
            lg = lg_ref[d, hd]
            decay = jnp.where(diff >= 0, jnp.exp(jnp.maximum(diff, 0.0) * lg), 0.0)
            xi = jnp.exp((tau + 1.0) * lg)
            zeta = jnp.exp((CHUNK - 1.0 - tau) * lg)
            g_end = jnp.exp(jnp.full((1, 1), CHUNK, f32) * lg)
            s_all.append((qk[H_A + hd] * decay).astype(bf16))
            kw.append(ks[H_A + hd] * zeta)
            post.append((xi, g_end))
        for hd in range(H_A):
            j = d * H_A + hd
            w_int, den, dec, m_new = post[hd]
            num = jnp.dot(s_all[hd], vs[hd], preferred_element_type=f32) + w_int * qst[hd]
            h_ref[:, hd * DV_A:(hd + 1) * DV_A] = num / den
            c_scr[j] = dec * c_scr[j] + lax.dot_general(kw[hd].astype(bf16), vs[hd], tn, preferred_element_type=f32)
            n_scr[j] = dec * n_scr[j] + jnp.sum(kw[hd], axis=0, keepdims=True)
            m_scr[j] = jnp.broadcast_to(m_new, (1, DK_A))
        for hd in range(H_B):
            j = d * H_B + hd
            xi, g_end = post[H_A + hd]
            o = jnp.dot(s_all[H_A + hd], vs[H_A + hd], preferred_element_type=f32) + xi * qst[H_A + hd]
            h_ref[:, W_A + hd * DV_B:W_A + (hd + 1) * DV_B] = o
            s_scr[j] = g_end * s_scr[j] + lax.dot_general(kw[H_A + hd].astype(bf16), vs[H_A + hd], tn, preferred_element_type=f32)

    @pl.when(ci == pl.num_programs(1) - 1)
    def _():
        cf_ref[0] = c_scr[...]
        nf_ref[0] = n_scr[...]
        mf_ref[0] = m_scr[...]
        sf_ref[0] = s_scr[...]


def _even_scan(proj, n_seq, seq_len, c0, n0, m0, s0, gate_bias, log_gamma):
    nc = seq_len // CHUNK
    f32 = jnp.float32
    st4 = lambda s, c: (s, 0, 0, 0)
    big = pl.BlockSpec((1, N_HD, DK_A, DV_A), st4)
    small = pl.BlockSpec((1, N_HD, 1, DK_A), st4)
    fwd = lambda s, c: (s * nc + c, 0)
    bwd = lambda s, c: (s * nc + (nc - 1 - c), 0)
    n = n_seq * seq_len
    return pl.pallas_call(
        _even_scan_kernel,
        grid=(n_seq, nc),
        in_specs=[pl.BlockSpec((CHUNK, EVEN_COLS), fwd), pl.BlockSpec((CHUNK, EVEN_COLS), bwd), big, small, small, big,
                  pl.BlockSpec((1, LANE), lambda s, c: (0, 0)), pl.BlockSpec(memory_space=pltpu.SMEM)],
        out_specs=[pl.BlockSpec((CHUNK, W_A + W_B), fwd), pl.BlockSpec((CHUNK, W_A + W_B), bwd), big, small, small, big],
        out_shape=[jax.ShapeDtypeStruct((n, W_A + W_B), f32), jax.ShapeDtypeStruct((n, W_A + W_B), f32),
                   jax.ShapeDtypeStruct(c0.shape, f32), jax.ShapeDtypeStruct(n0.shape, f32), jax.ShapeDtypeStruct(m0.shape, f32), jax.ShapeDtypeStruct(s0.shape, f32)],
        scratch_shapes=[pltpu.VMEM((N_HD, DK_A, DV_A), f32), pltpu.VMEM((N_HD, 1, DK_A), f32), pltpu.VMEM((N_HD, 1, DK_A), f32), pltpu.VMEM((N_HD, DK_B, DV_B), f32)],
        name="even_scan",
    )(proj, proj, c0, n0, m0, s0, gate_bias, log_gamma)


def _even_post_kernel(hf_ref, hb_ref, oa_ref, gb_ref, x_ref, g1_ref, gn_ref, w_ref, lng_ref, lnb_ref, o_ref):
    hs = hf_ref[...] + hb_ref[...]
    parts = []
    for hd in range(H_A + H_B):
        hh = hs[:, hd * DV_A:(hd + 1) * DV_A]
        mu = jnp.mean(hh, axis=-1, keepdims=True)
        hc = hh - mu
        var = jnp.mean(hc * hc, axis=-1, keepdims=True)
        parts.append(hc * lax.rsqrt(var + 1e-5))
    normed = jnp.concatenate(parts, axis=1) * gn_ref[...]
    gb = gb_ref[...]
    act = jnp.concatenate([jax.nn.sigmoid(oa_ref[...]), gb * jax.nn.sigmoid(gb)], axis=1)
    ms = jnp.dot((normed * act).astype(jnp.bfloat16), w_ref[...], preferred_element_type=jnp.float32)
    o_ref[...] = _layer_norm_rows(ALPHA * x_ref[...] + g1_ref[0] * ms, lng_ref[...], lnb_ref[...])


def _even_post(hf, hb, proj, x, g1, gn, w_out_b, ln_g, ln_b):
    n_seq, seq_len, d = x.shape
    n = n_seq * seq_len
    per_seq = seq_len // ROW_TILE
    row = lambda i: (i, 0)
    full = lambda i: (0, 0)
    return pl.pallas_call(
        _even_post_kernel,
        grid=(n // ROW_TILE,),
        in_specs=[pl.BlockSpec((ROW_TILE, W_A + W_B), row), pl.BlockSpec((ROW_TILE, W_A + W_B), row),
                  pl.BlockSpec((ROW_TILE, W_A), lambda i: (i, 3)), pl.BlockSpec((ROW_TILE, W_B), lambda i: (i, 7)),
                  pl.BlockSpec((ROW_TILE, d), row), pl.BlockSpec((1, 1, d), lambda i: (i // per_seq, 0, 0)),
                  pl.BlockSpec((1, W_A + W_B), full), pl.BlockSpec((W_A + W_B, d), full), pl.BlockSpec((1, d), full), pl.BlockSpec((1, d), full)],
        out_specs=pl.BlockSpec((ROW_TILE, d), row),
        out_shape=jax.ShapeDtypeStruct((n, d), jnp.float32),
        name="even_post",
    )(hf, hb, proj, proj, x.reshape(n, d), g1, gn, w_out_b, ln_g, ln_b).reshape(x.shape)


def _even_weights(w_in, i_bias, f_bias, log_decay, gn_a, gn_b, w_out):
    qa, ka, va, oa, ia, fa, qb, kb, vb, gb = jnp.split(w_in, EVEN_SPLITS, axis=1)
    gates = jnp.pad(jnp.concatenate([ia, fa], axis=1), ((0, 0), (0, LANE - 2 * N_HD)))
    w_in_b = jnp.concatenate([qa, ka, va, oa, qb, kb, vb, gb, gates], axis=1).astype(jnp.bfloat16)
    gate_bias = jnp.pad(jnp.concatenate([i_bias.reshape(-1), f_bias.reshape(-1)]), (0, LANE - 2 * N_HD)).reshape(1, LANE)
    log_gamma = -jnp.exp(log_decay.astype(jnp.float32))
    gn = jnp.concatenate([gn_a.reshape(-1), gn_b.reshape(-1)]).reshape(1, W_A + W_B)
    return w_in_b, gate_bias, log_gamma, gn, w_out.astype(jnp.bfloat16)


def _even_layer(x, sc1, sh1, g1, wts, ln_g, ln_b, c0, n0, m0, s0):
    w_in_b, gate_bias, log_gamma, gn, w_out_b = wts
    n_seq, seq_len, d = x.shape
    proj = _mod_proj(x, sc1, sh1, w_in_b)
    st = lambda a: a.reshape(n_seq, N_HD, DK_A, DV_A)
    vec = lambda a: a.reshape(n_seq, N_HD, 1, DK_A)
    m0b = jnp.broadcast_to(m0.reshape(n_seq, N_HD, 1, 1), (n_seq, N_HD, 1, DK_A))
    hf, hb, cf, nf, mf, sf = _even_scan(proj, n_seq, seq_len, st(c0), vec(n0), m0b, st(s0), gate_bias, log_gamma)
    x_new = _even_post(hf, hb, proj, x, g1, gn, w_out_b, ln_g.reshape(1, d), ln_b.reshape(1, d))
    fin = (cf.reshape(n_seq, N_DIR, H_A, DK_A, DV_A), nf.reshape(n_seq, N_DIR, H_A, DK_A), mf[:, :, 0, 0].reshape(n_seq, N_DIR, H_A), sf.reshape(n_seq, N_DIR, H_B, DK_B, DV_B))
    return x_new, fin


ODD_DT = D_INNER + CONV_DIM
ODD_COLS = ODD_DT + LANE
CONV_TILE = 1024
HALO = 8
GW = J_C * P_C


def _conv_kernel(x_ref, prev_ref, next_ref, w_ref, b_ref, o_ref, *, per_seq):
    i = pl.program_id(0)
    pad = D_CONV // 2
    rows = x_ref.shape[0]
    first = lax.rem(i, per_seq) == 0
    last = lax.rem(i, per_seq) == per_seq - 1
    prev = jnp.where(first, 0.0, prev_ref[HALO - pad:, :])
    nxt = jnp.where(last, 0.0, next_ref[:pad, :])
    ext = jnp.concatenate([prev, x_ref[...], nxt], axis=0)
    acc = b_ref[...] + ext[0:rows] * w_ref[0:1, :]
    for k in range(1, D_CONV):
        acc = acc + ext[k:k + rows] * w_ref[k:k + 1, :]
    o_ref[...] = acc * jax.nn.sigmoid(acc)


def _conv_silu(proj, seq_len, conv_w, conv_b):
    n = proj.shape[0]
    per_seq = seq_len // ROW_TILE
    c0 = D_INNER // CONV_TILE
    hb = ROW_TILE // HALO
    n_hb = n // HALO
    return pl.pallas_call(
        functools.partial(_conv_kernel, per_seq=per_seq),
        grid=(n // ROW_TILE, CONV_DIM // CONV_TILE),
        in_specs=[pl.BlockSpec((ROW_TILE, CONV_TILE), lambda i, j: (i, c0 + j)),
                  pl.BlockSpec((HALO, CONV_TILE), lambda i, j: (jnp.maximum(i * hb - 1, 0), c0 + j)),
                  pl.BlockSpec((HALO, CONV_TILE), lambda i, j: (jnp.minimum((i + 1) * hb, n_hb - 1), c0 + j)),
                  pl.BlockSpec((D_CONV, CONV_TILE), lambda i, j: (0, j)), pl.BlockSpec((1, CONV_TILE), lambda i, j: (0, j))],
        out_specs=pl.BlockSpec((ROW_TILE, CONV_TILE), lambda i, j: (i, j)),
        out_shape=jax.ShapeDtypeStruct((n, CONV_DIM), jnp.float32),
        name="conv_silu",
    )(proj, proj, proj, conv_w, conv_b.reshape(1, CONV_DIM))


def _spread_heads(mat, col0):
    lane = lax.broadcasted_iota(jnp.int32, (mat.shape[0], LANE), 1)
    per_tile = LANE // P_C
    tiles = []
    for j in range(0, J_C, per_tile):
        t = mat[:, col0 + j:col0 + j + 1]
        for r in range(1, per_tile):
            t = jnp.where(lane < r * P_C, t, mat[:, col0 + j + r:col0 + j + r + 1])
        tiles.append(jnp.broadcast_to(t, (mat.shape[0], LANE)))
    return jnp.concatenate(tiles, axis=1)


def _ssd_scan_kernel(af_ref, ab_ref, dtf_ref, dtb_ref, bias_ref, alog_ref, h0_ref, yf_ref, yb_ref, hf_ref, h_scr):
    f32, bf16 = jnp.float32, jnp.bfloat16
    ci = pl.program_id(1)

    @pl.when(ci == 0)
    def _():
        h_scr[...] = h0_ref[0]

    t_i = lax.broadcasted_iota(jnp.int32, (CHUNK, CHUNK), 0)
    s_i = lax.broadcasted_iota(jnp.int32, (CHUNK, CHUNK), 1)
    nt = (((1,), (1,)), ((), ()))
    tn = (((0,), (0,)), ((), ()))
    hi = lax.Precision.HIGHEST
    a_neg = -jnp.exp(alog_ref[...])

    for d, (x_ref, dt_ref, y_ref) in enumerate(((af_ref, dtf_ref, yf_ref), (ab_ref, dtb_ref, yb_ref))):
        mask = (s_i <= t_i) if d == 0 else (s_i >= t_i)
        tri = mask.astype(f32)
        last = CHUNK - 1 if d == 0 else 0
        dtv = jax.nn.softplus(dt_ref[...] + bias_ref[...])
        a = dtv * a_neg
        ac_cols = jnp.dot(tri, a, precision=hi, preferred_element_type=f32)
        ac_rows = lax.dot_general(a.T, tri, nt, precision=hi, preferred_element_type=f32)
        dt_rows = dtv.T
        e_cols = jnp.exp(ac_cols)
        w_cols = jnp.exp(ac_cols[last:last + 1] - ac_cols) * dtv
        bgs, xgs, cbs, inters = [], [], [], []
        for g in range(G_C):
            bg = x_ref[:, D_INNER + g * N_C:D_INNER + (g + 1) * N_C].astype(bf16)
            cg = x_ref[:, D_INNER + G_C * N_C + g * N_C:D_INNER + G_C * N_C + (g + 1) * N_C].astype(bf16)
            bgs.append(bg)
            xgs.append(x_ref[:, g * GW:(g + 1) * GW])
            cbs.append(lax.dot_general(cg, bg, nt, preferred_element_type=f32))
            inters.append(jnp.dot(cg, h_scr[d, g].astype(bf16), preferred_element_type=f32))
        s_all = []
        for g in range(G_C):
            for j in range(J_C):
                col = d * H_C + g * J_C + j
                seg = jnp.where(mask, ac_cols[:, col:col + 1] - ac_rows[col:col + 1, :], -jnp.inf)
                s_all.append((cbs[g] * jnp.exp(seg) * dt_rows[col:col + 1, :]).astype(bf16))
        intras, upds, e_gs = [], [], []
        for g in range(G_C):
            col0 = d * H_C + g * J_C
            e_gs.append(_spread_heads(e_cols, col0))
            intras.append([jnp.dot(s_all[g * J_C + j], xgs[g][:, j * P_C:(j + 1) * P_C].astype(bf16), preferred_element_type=f32) for j in range(J_C)])
            xw = (xgs[g] * _spread_heads(w_cols, col0)).astype(bf16)
            upds.append(lax.dot_general(bgs[g], xw, tn, preferred_element_type=f32))
        for g in range(G_C):
            y_ref[:, g * GW:(g + 1) * GW] = jnp.concatenate(intras[g], axis=1) + e_gs[g] * inters[g]
            h_scr[d, g] = e_gs[g][last:last + 1] * h_scr[d, g] + upds[g]

    @pl.when(ci == pl.num_programs(1) - 1)
    def _():
        hf_ref[0] = h_scr[...]


def _ssd_chunk_scan(xa, proj, n_seq, seq_len, dt_bias_row, a_log_row, h0):
    nc = seq_len // CHUNK
    n = n_seq * seq_len
    f32 = jnp.float32
    fwd = lambda s, c: (s * nc + c, 0)
    bwd = lambda s, c: (s * nc + (nc - 1 - c), 0)
    dcol = ODD_DT // LANE
    st = pl.BlockSpec((1, N_DIR, G_C, N_C, GW), lambda s, c: (s, 0, 0, 0, 0))
    row = pl.BlockSpec((1, LANE), lambda s, c: (0, 0))
    return pl.pallas_call(
        _ssd_scan_kernel,
        grid=(n_seq, nc),
        in_specs=[pl.BlockSpec((CHUNK, CONV_DIM), fwd), pl.BlockSpec((CHUNK, CONV_DIM), bwd),
                  pl.BlockSpec((CHUNK, LANE), lambda s, c: (s * nc + c, dcol)), pl.BlockSpec((CHUNK, LANE), lambda s, c: (s * nc + (nc - 1 - c), dcol)),
                  row, row, st],
        out_specs=[pl.BlockSpec((CHUNK, D_INNER), fwd), pl.BlockSpec((CHUNK, D_INNER), bwd), st],
        out_shape=[jax.ShapeDtypeStruct((n, D_INNER), f32), jax.ShapeDtypeStruct((n, D_INNER), f32), jax.ShapeDtypeStruct(h0.shape, f32)],
        scratch_shapes=[pltpu.VMEM((N_DIR, G_C, N_C, GW), f32)],
        name="ssd_scan",
    )(xa, xa, proj, proj, dt_bias_row, a_log_row, h0)


def _odd_post_kernel(yf_ref, yb_ref, xs_ref, z_ref, x_ref, g1_ref, dsk_ref, nw_ref, w_ref, lng_ref, lnb_ref, o_ref):
    z = z_ref[...]
    y = (yf_ref[...] + yb_ref[...] + dsk_ref[...] * xs_ref[...]) * (z * jax.nn.sigmoid(z))
    y = y * lax.rsqrt(jnp.mean(y * y, axis=-1, keepdims=True) + 1e-5) * nw_ref[...]
    ms = jnp.dot(y.astype(jnp.bfloat16), w_ref[...], preferred_element_type=jnp.float32)
    o_ref[...] = _layer_norm_rows(ALPHA * x_ref[...] + g1_ref[0] * ms, lng_ref[...], lnb_ref[...])


def _odd_post(yf, yb, xa, proj, x, g1, d_row, norm_w, w_out_b, ln_g, ln_b):
    n_seq, seq_len, d = x.shape
    n = n_seq * seq_len
    per_seq = seq_len // ROW_TILE
    row = lambda i: (i, 0)
    full = lambda i: (0, 0)
    wide = pl.BlockSpec((ROW_TILE, D_INNER), row)
    return pl.pallas_call(
        _odd_post_kernel,
        grid=(n // ROW_TILE,),
        in_specs=[wide, wide, wide, wide, pl.BlockSpec((ROW_TILE, d), row), pl.BlockSpec((1, 1, d), lambda i: (i // per_seq, 0, 0)),
                  pl.BlockSpec((1, D_INNER), full), pl.BlockSpec((1, D_INNER), full), pl.BlockSpec((D_INNER, d), full), pl.BlockSpec((1, d), full), pl.BlockSpec((1, d), full)],
        out_specs=pl.BlockSpec((ROW_TILE, d), row),
        out_shape=jax.ShapeDtypeStruct((n, d), jnp.float32),
        name="odd_post",
    )(yf, yb, xa, proj, x.reshape(n, d), g1, d_row, norm_w, w_out_b, ln_g, ln_b).reshape(x.shape)


def _odd_weights(w_in, dt_bias, a_log, d_skip, norm_w, w_out):
    w_in_b = jnp.pad(w_in, ((0, 0), (0, ODD_COLS - w_in.shape[1]))).astype(jnp.bfloat16)
    lane_row = lambda a: jnp.pad(a.astype(jnp.float32).reshape(-1), (0, LANE - N_DIR * H_C)).reshape(1, LANE)
    d_row = jnp.repeat(d_skip.astype(jnp.float32), P_C).reshape(1, D_INNER)
    return w_in_b, lane_row(dt_bias), lane_row(a_log), d_row, norm_w.reshape(1, D_INNER), w_out.astype(jnp.bfloat16)


def _odd_layer(x, sc1, sh1, g1, wts, conv_w, conv_b, ln_g, ln_b, h0):
    w_in_b, dt_row, alog_row, d_row, nw_row, w_out_b = wts
    n_seq, seq_len, d = x.shape
    proj = _mod_proj(x, sc1, sh1, w_in_b)
    xa = _conv_silu(proj, seq_len, conv_w, conv_b)
    to_scan = lambda h: h.reshape(n_seq, N_DIR, G_C, J_C, P_C, N_C).transpose(0, 1, 2, 5, 3, 4).reshape(n_seq, N_DIR, G_C, N_C, GW)
    yf, yb, hfin = _ssd_chunk_scan(xa, proj, n_seq, seq_len, dt_row, alog_row, to_scan(h0))
    x_new = _odd_post(yf, yb, xa, proj, x, g1, d_row, nw_row, w_out_b, ln_g.reshape(1, d), ln_b.reshape(1, d))
    fin = hfin.reshape(n_seq, N_DIR, G_C, N_C, J_C, P_C).transpose(0, 1, 2, 4, 5, 3).reshape(n_seq, N_DIR, H_C, P_C, N_C)
    return x_new, fin


def _modulation(cvec, w, b):
    m = jax.nn.silu(cvec) @ w + b
    return [t[:, None, :] for t in jnp.split(m, 6, axis=-1)]


def _grid_pos_embed(n_tok, dtype):
    rows = n_tok // GRID_W
    r = jnp.repeat(jnp.arange(rows, dtype=jnp.float32), GRID_W)
    col = jnp.tile(jnp.arange(GRID_W, dtype=jnp.float32), rows)
    quarter = D_MODEL // 4
    freqs = 1.0 / (10000.0 ** (jnp.arange(quarter, dtype=jnp.float32) / quarter))
    er = r[:, None] * freqs
    ec = col[:, None] * freqs
    return jnp.concatenate([jnp.sin(er), jnp.cos(er), jnp.sin(ec), jnp.cos(ec)], axis=-1).astype(dtype)


def _expert_slab_copy(tab_ref, buf_ref, sem_ref, row, pair, slot):
    src = tab_ref.at[pl.ds(pl.multiple_of(row, SLAB_ROWS), SLAB_ROWS), :]
    return pltpu.make_async_copy(src, buf_ref.at[slot, pl.ds(pair * SLAB_PITCH, SLAB_ROWS), :], sem_ref.at[slot])


def _peer_gather_kernel(idx_ref, x_ref, g_ref, tab_ref, o_ref, buf_ref, sem_ref):
    n_tok = x_ref.shape[0]
    n_chunk = x_ref.shape[1] // LANE

    per_phase = PEER_PAIRS // (2 * n_chunk)

    def issue(t, slot, p0, p1):
        for p in range(p0, p1):
            _expert_slab_copy(tab_ref, buf_ref, sem_ref, idx_ref[t, p], p, slot).start(priority=p % 2)

    def wait(slot):
        n_rows = PEER_PAIRS * SLAB_ROWS
        pltpu.make_async_copy(tab_ref.at[pl.ds(0, n_rows), :], buf_ref.at[slot, pl.ds(0, n_rows), :], sem_ref.at[slot]).wait()

    def chunk(slot, c):
        return buf_ref[slot, pl.ds(c, PEER_PAIRS, stride=SLAB_PITCH), :]

    eye = lax.broadcasted_iota(jnp.int32, (PEER_PAIRS, PEER_PAIRS), 0) == lax.broadcasted_iota(jnp.int32, (PEER_PAIRS, PEER_PAIRS), 1)

    def compute(t, slot, t_next, slot_next):
        phase = [0]

        def issue_some():
            if t_next is not None:
                issue(t_next, slot_next, phase[0] * per_phase, (phase[0] + 1) * per_phase)
            phase[0] += 1

        wait(slot)
        gt = g_ref[pl.ds(t, 1), :]
        xt = x_ref[pl.ds(t, 1), :]
        acc = chunk(slot, 0) * xt[:, :LANE]
        issue_some()
        for c in range(1, n_chunk):
            acc = acc + chunk(slot, c) * xt[:, c * LANE:(c + 1) * LANE]
            issue_some()
        act = jnp.sum(acc, axis=1, keepdims=True)
        g_col = jnp.sum(jnp.where(eye, gt, 0.0), axis=1, keepdims=True)
        w = g_col * (0.5 * act * (1.0 + lax.erf(act * np.float32(np.sqrt(0.5)))))
        outs = []
        for c in range(n_chunk):
            outs.append(jnp.sum(w * chunk(slot, n_chunk + c), axis=0, keepdims=True))
            issue_some()
        o_ref[pl.ds(t, 1), :] = jnp.concatenate(outs, axis=1)

    ahead = PEER_NBUF - 1
    for s in range(ahead):
        issue(s, s, 0, PEER_PAIRS)

    def group(base, n_issue):
        for j in range(PEER_NBUF):
            if j < n_issue:
                compute(base + j, j, base + j + ahead, (j + ahead) % PEER_NBUF)
            else:
                compute(base + j, j, None, None)

    n_group = n_tok // PEER_NBUF

    def steady(gi, carry):
        group(gi * PEER_NBUF, PEER_NBUF)
        return carry

    lax.fori_loop(0, n_group - 1, steady, 0)
    group((n_group - 1) * PEER_NBUF, PEER_NBUF - ahead)


def _peer_gather(x, idx, g, uv_tab):
    n, d = x.shape
    return pl.pallas_call(
        _peer_gather_kernel,
        grid=(n // PEER_TB,),
        in_specs=[
            pl.BlockSpec((PEER_TB, PEER_PAIRS), lambda i: (i, 0), memory_space=pltpu.SMEM),
            pl.BlockSpec((PEER_TB, d), lambda i: (i, 0)),
            pl.BlockSpec((PEER_TB, PEER_PAIRS), lambda i: (i, 0)),
            pl.BlockSpec(memory_space=pl.ANY),
        ],
        out_specs=pl.BlockSpec((PEER_TB, d), lambda i: (i, 0)),
        out_shape=jax.ShapeDtypeStruct((n, d), jnp.float32),
        scratch_shapes=[pltpu.VMEM((PEER_NBUF, PEER_PAIRS * SLAB_PITCH, LANE), jnp.float32), pltpu.SemaphoreType.DMA((PEER_NBUF,))],
        name="peer_gather",
    )(idx * SLAB_ROWS, x, g, uv_tab)


def _top_rows(s, k, payload=None):
    (vals, sel), = _top_rows_many([s], k, None if payload is None else [payload])
    return vals, sel


def _top_rows_many(ss, k, payloads=None):
    n_rows, width = ss[0].shape
    row = lax.broadcasted_iota(jnp.int32, (n_rows, width), 0)
    out_row = lax.broadcasted_iota(jnp.int32, (k, width), 0)
    ss = list(ss)
    vals = [jnp.zeros((k, width), jnp.float32) for _ in ss]
    sel = [jnp.zeros((k, width), jnp.int32) for _ in ss]
    for j in range(k):
        for i in range(len(ss)):
            m = jnp.max(ss[i], axis=0, keepdims=True)
            r = jnp.min(jnp.where(ss[i] == m, row, n_rows), axis=0, keepdims=True)
            hit = row == r
            ss[i] = jnp.where(hit, -jnp.inf, ss[i])
            what = r if payloads is None else jnp.max(jnp.where(hit, payloads[i], -1), axis=0, keepdims=True)
            vals[i] = jnp.where(out_row == j, m, vals[i])
            sel[i] = jnp.where(out_row == j, what, sel[i])
    return list(zip(vals, sel))


def _peer_select_kernel(x_ref, sc_ref, sh_ref, wq_ref, keys_ref, xm_ref, idx_ref, g_ref, q_scr, sv_scr, si_scr, bi_scr, bg_scr):
    f32 = jnp.float32
    n_hc = 2 * H_P
    h = x_ref[...] * (1.0 + sc_ref[0]) + sh_ref[0]
    xm_ref[...] = h
    q = jnp.dot(h.astype(jnp.bfloat16), wq_ref[...], preferred_element_type=f32)
    for hc in range(n_hc):
        q_scr[hc] = q[:, hc * D_HALF:(hc + 1) * D_HALF].astype(jnp.bfloat16)

    def sub_key_top(hd, carry):
        hcs = (2 * hd, 2 * hd + 1)
        ss = [lax.dot_general(keys_ref[hc], q_scr[hc], (((1,), (1,)), ((), ())), preferred_element_type=f32) for hc in hcs]
        for hc, (v, i) in zip(hcs, _top_rows_many(ss, TOPK)):
            sv_scr[hc], si_scr[hc] = v, i
        return carry

    lax.fori_loop(0, H_P, sub_key_top, 0)

    def pair_top(hd, carry):
        sv0, sv1 = sv_scr[2 * hd], sv_scr[2 * hd + 1]
        si0, si1 = si_scr[2 * hd], si_scr[2 * hd + 1]
        half = TOPK // 2
        cand, cidx, n_cand = [], [], 0
        for a in range(half):
            n_b = TOPK // (a + 1)
            cand.append(sv0[a:a + 1] + sv1[:n_b])
            cidx.append(si0[a:a + 1] * N_KEYS + si1[:n_b])
            n_cand += n_b
        n_fill = -n_cand % 8
        cand.append(jnp.full((n_fill, sv0.shape[1]), -jnp.inf, f32))
        cidx.append(jnp.full((n_fill, sv0.shape[1]), -1, jnp.int32))
        cand.append(sv0[half:] + sv1[0:1])
        cidx.append(si0[half:] * N_KEYS + si1[0:1])
        best, eid = _top_rows(jnp.concatenate(cand, axis=0), TOPK, payload=jnp.concatenate(cidx, axis=0))
        e = jnp.exp(best - best[0:1])
        bi_scr[hd] = eid
        bg_scr[hd] = e / jnp.sum(e, axis=0, keepdims=True)
        return carry

    lax.fori_loop(0, H_P, pair_top, 0)
    tm = x_ref.shape[0]
    idx_ref[...] = bi_scr[...].reshape(PEER_PAIRS, tm).T
    g_ref[...] = bg_scr[...].reshape(PEER_PAIRS, tm).T


def _peer_select(x, sc, sh, w_q, sub_keys):
    n_seq, seq_len, d = x.shape
    n = n_seq * seq_len
    n_hc = 2 * H_P
    per_seq = seq_len // SEL_TM
    tok = lambda i: (i, 0)
    seq = lambda i: (i // per_seq, 0, 0)
    return pl.pallas_call(
        _peer_select_kernel,
        grid=(n // SEL_TM,),
        in_specs=[
            pl.BlockSpec((SEL_TM, d), tok),
            pl.BlockSpec((1, 1, d), seq),
            pl.BlockSpec((1, 1, d), seq),
            pl.BlockSpec((d, H_P * D_QK), lambda i: (0, 0)),
            pl.BlockSpec((n_hc, N_KEYS, D_HALF), lambda i: (0, 0, 0)),
        ],
        out_specs=[pl.BlockSpec((SEL_TM, d), tok), pl.BlockSpec((SEL_TM, PEER_PAIRS), tok), pl.BlockSpec((SEL_TM, PEER_PAIRS), tok)],
        out_shape=[jax.ShapeDtypeStruct((n, d), jnp.float32), jax.ShapeDtypeStruct((n, PEER_PAIRS), jnp.int32), jax.ShapeDtypeStruct((n, PEER_PAIRS), jnp.float32)],
        scratch_shapes=[
            pltpu.VMEM((n_hc, SEL_TM, D_HALF), jnp.bfloat16),
            pltpu.VMEM((n_hc, TOPK, SEL_TM), jnp.float32),
            pltpu.VMEM((n_hc, TOPK, SEL_TM), jnp.int32),
            pltpu.VMEM((H_P, TOPK, SEL_TM), jnp.int32),
            pltpu.VMEM((H_P, TOPK, SEL_TM), jnp.float32),
        ],
        name="peer_select",
    )(x.reshape(n, d), sc, sh, w_q.astype(jnp.bfloat16), sub_keys.reshape(n_hc, N_KEYS, D_HALF).astype(jnp.bfloat16))


def _residual_norm_kernel(x_ref, f_ref, g_ref, lng_ref, lnb_ref, o_ref):
    o_ref[...] = _layer_norm_rows(ALPHA * x_ref[...] + g_ref[0] * f_ref[...], lng_ref[...], lnb_ref[...])


def _residual_norm(x, f, gate, ln_g, ln_b):
    n_seq, seq_len, d = x.shape
    n = n_seq * seq_len
    per_seq = seq_len // ROW_TILE
    row = pl.BlockSpec((ROW_TILE, d), lambda i: (i, 0))
    vec = pl.BlockSpec((1, d), lambda i: (0, 0))
    return pl.pallas_call(
        _residual_norm_kernel,
        grid=(n // ROW_TILE,),
        in_specs=[row, row, pl.BlockSpec((1, 1, d), lambda i: (i // per_seq, 0, 0)), vec, vec],
        out_specs=row,
        out_shape=jax.ShapeDtypeStruct((n, d), jnp.float32),
        name="residual_norm",
    )(x.reshape(n, d), f.reshape(n, d), gate, ln_g.reshape(1, d), ln_b.reshape(1, d)).reshape(x.shape)


def _peer(x, sc, sh, w_q, sub_keys, uv_tab):
    xm, idx, g = _peer_select(x, sc, sh, w_q, sub_keys)
    return _peer_gather(xm, idx, g, uv_tab).reshape(x.shape)


def kernel(x_prompt, x_sample, c, c_ctx, state_mlstm_C, state_mlstm_n, state_mlstm_m, state_ret, state_ssd, w_mod, b_mod, ln1_g, ln1_b, ln2_g, ln2_b, even_w_in, mlstm_i_bias, mlstm_f_bias, ret_log_decay, even_gn_a, even_gn_b, even_w_out, odd_w_in, odd_conv_w, odd_conv_b, odd_dt_bias, odd_a_log, odd_d, odd_norm_w, odd_w_out, peer_w_q, peer_sub_keys, peer_u, peer_v):
    f32 = jnp.float32
    bp = x_prompt.shape[0]
    xp = x_prompt
    xs = x_sample + _grid_pos_embed(x_sample.shape[1], x_sample.dtype)[None]
    new_C, new_n, new_m, new_S, new_h = [], [], [], [], []
    bcast = lambda m: jnp.broadcast_to(m, (bp, 1, D_MODEL))
    for l in range(DEPTH):
        sh1p, sc1p, g1p, sh2p, sc2p, g2p = [bcast(m) for m in _modulation(c_ctx[None, :], w_mod[l], b_mod[l])]
        sh1s, sc1s, g1s, sh2s, sc2s, g2s = _modulation(c, w_mod[l], b_mod[l])
        j = l // 2
        if l % 2 == 0:
            wts = _even_weights(even_w_in[j], mlstm_i_bias[j], mlstm_f_bias[j], ret_log_decay[j], even_gn_a[j], even_gn_b[j], even_w_out[j])
            zero = (jnp.zeros((bp, N_DIR, H_A, DK_A, DV_A), f32), jnp.zeros((bp, N_DIR, H_A, DK_A), f32), jnp.zeros((bp, N_DIR, H_A), f32), jnp.zeros((bp, N_DIR, H_B, DK_B, DV_B), f32))
            xp, fin = _even_layer(xp, sc1p, sh1p, g1p, wts, ln1_g[l], ln1_b[l], *zero)
            xs, _ = _even_layer(xs, sc1s, sh1s, g1s, wts, ln1_g[l], ln1_b[l], state_mlstm_C[:, j], state_mlstm_n[:, j], state_mlstm_m[:, j], state_ret[:, j])
            new_C.append(fin[0])
            new_n.append(fin[1])
            new_m.append(fin[2])
            new_S.append(fin[3])
        else:
            wts = _odd_weights(odd_w_in[j], odd_dt_bias[j], odd_a_log[j], odd_d[j], odd_norm_w[j], odd_w_out[j])
            xp, fin = _odd_layer(xp, sc1p, sh1p, g1p, wts, odd_conv_w[j], odd_conv_b[j], ln1_g[l], ln1_b[l], jnp.zeros((bp, N_DIR, H_C, P_C, N_C), f32))
            xs, _ = _odd_layer(xs, sc1s, sh1s, g1s, wts, odd_conv_w[j], odd_conv_b[j], ln1_g[l], ln1_b[l], state_ssd[:, j])
            new_h.append(fin)
        uv_tab = jnp.concatenate([peer_u[l], peer_v[l]], axis=1).reshape(-1, LANE)
        fp = _peer(xp, sc2p, sh2p, peer_w_q[l], peer_sub_keys[l], uv_tab)
        fs = _peer(xs, sc2s, sh2s, peer_w_q[l], peer_sub_keys[l], uv_tab)
        xp = _residual_norm(xp, fp, g2p, ln2_g[l], ln2_b[l])
        xs = _residual_norm(xs, fs, g2s, ln2_g[l], ln2_b[l])
    return (xp, xs, jnp.stack(new_C, axis=1), jnp.stack(new_n, axis=1), jnp.stack(new_m, axis=1), jnp.stack(new_S, axis=1), jnp.stack(new_h, axis=1))
```
